```python
import jax, jax.numpy as jnp
from jax import lax
import numpy as np

D_MODEL = 1024
BATCH = 32
SEQ = 2048
DEPTH = 1
DEC_BATCH = 32
DEC_SEQ = 64
PAST_LEN = 4096

CHUNK = 64
MLP_CHUNK = 128
D_MIX = 2 * D_MODEL
D_A = D_MIX // 2
HEAD_A = 128
N_HEADS_A = D_A // HEAD_A
D_B = D_MIX - D_A
POOL_WINDOWS = (2, 4, 8, 16)
N_POOL_GROUPS = len(POOL_WINDOWS)
POOL_GROUP = D_B // N_POOL_GROUPS
POOL_HIST = max(POOL_WINDOWS) - 1
D_PLE = 256
D_IN = 3 * D_A + 2 * D_B
ALPHA = (2.0 * DEPTH) ** 0.25
BETA = (8.0 * DEPTH) ** -0.25
LN_EPS = 1e-5

kernel_name = "hybrid_chunkmlp_pool_stream_step"


def _layernorm(x, g, b):
    xf = x.astype(jnp.float32)
    mu = jnp.mean(xf, axis=-1, keepdims=True)
    var = jnp.mean(jnp.square(xf - mu), axis=-1, keepdims=True)
    y = (xf - mu) * lax.rsqrt(var + LN_EPS)
    return (y * g.astype(jnp.float32) + b.astype(jnp.float32)).astype(x.dtype)


def _chunk_spatial(v, w_s, b_s):
    B, T, H, Dh = v.shape
    n = -(-T // MLP_CHUNK)
    pad = n * MLP_CHUNK - T
    if pad:
        v = jnp.pad(v, ((0, 0), (0, pad), (0, 0), (0, 0)))
    vc = v.reshape(B, n, MLP_CHUNK, H, Dh)
    blk = np.arange(MLP_CHUNK) // CHUNK
    mask = jnp.asarray(blk[None, :] <= blk[:, None])
    w = jnp.where(mask[None], w_s, jnp.zeros((), w_s.dtype))
    out = jnp.einsum('hij,bnjhd->bnihd', w, vc) + jnp.transpose(b_s)[None, None, :, :, None]
    return out.reshape(B, n * MLP_CHUNK, H, Dh)[:, :T]


def _multiscale_pool(h, hist, pos0):
    T = h.shape[1]
    z = jnp.concatenate([hist, h], axis=1).astype(jnp.float32)
    c = jnp.cumsum(z, axis=1)
    c = jnp.pad(c, ((0, 0), (1, 0), (0, 0)))
    pos = pos0 + jnp.arange(T)
    means = []
    for g, w in enumerate(POOL_WINDOWS):
        sl = slice(g * POOL_GROUP, (g + 1) * POOL_GROUP)
        s = c[:, POOL_HIST + 1:POOL_HIST + 1 + T, sl] - c[:, POOL_HIST + 1 - w:POOL_HIST + 1 - w + T, sl]
        cnt = jnp.minimum(pos + 1, w).astype(jnp.float32)
        means.append(s / cnt[None, :, None])
    mean = jnp.concatenate(means, axis=-1)
    return (mean - h.astype(jnp.float32)).astype(h.dtype)


def _layer(x, p, hist_pool, pos0, w_in, ln_v_g, ln_v_b, w_s, b_s, pool_w, pool_scale,
           w_out, b_out, ln_g, ln_b, w_ple, w_pg, b_pg):
    B, T, _ = x.shape
    h = jnp.einsum('btd,de->bte', x, w_in)
    u, v, ga, hp, gb = jnp.split(h, [D_A, 2 * D_A, 3 * D_A, 3 * D_A + D_B], axis=-1)
    v = _layernorm(v.reshape(B, T, N_HEADS_A, HEAD_A), ln_v_g, ln_v_b)
    mixed = _chunk_spatial(v, w_s, b_s).reshape(B, T, D_A)
    a_out = u * mixed * jax.nn.silu(ga)
    pooled = _multiscale_pool(hp, hist_pool, pos0).reshape(B, T, N_POOL_GROUPS, POOL_GROUP)
    pooled = jnp.einsum('btgc,gce->btge', pooled, pool_w).reshape(B, T, D_B)
    b_outp = pooled * pool_scale * jax.nn.silu(gb)
    y = jnp.einsum('bte,ed->btd', jnp.concatenate([a_out, b_outp], axis=-1), w_out) + b_out
    x = _layernorm(ALPHA * x + y, ln_g, ln_b)
    gate = jax.nn.sigmoid(jnp.einsum('btd,de->bte', x, w_pg) + b_pg)
    x = x + gate * jnp.einsum('btk,kd->btd', p, w_ple)
    new_hist = jnp.concatenate([hist_pool, hp], axis=1)[:, -POOL_HIST:]
    return x, v, new_hist


def setup_inputs(seed: int = 0) -> dict:
    key = jax.random.key(seed)
    ks = jax.random.split(key, 20)
    f = jnp.float32
    nrm = lambda k, s: jax.random.normal(k, s, f)
    return {
        "x_prompt": nrm(ks[0], (BATCH, SEQ, D_MODEL)),
        "x_sample": nrm(ks[1], (DEC_BATCH, DEC_SEQ, D_MODEL)),
        "state_pool": nrm(ks[2], (DEPTH, DEC_BATCH, POOL_HIST, D_B)),
        "p_prompt": nrm(ks[3], (DEPTH, BATCH, SEQ, D_PLE)),
        "p_sample": nrm(ks[4], (DEPTH, DEC_BATCH, DEC_SEQ, D_PLE)),
        "w_in": nrm(ks[5], (DEPTH, D_MODEL, D_IN)) * D_MODEL ** -0.5,
        "ln_v_g": 1.0 + 0.02 * nrm(ks[6], (DEPTH, N_HEADS_A, HEAD_A)),
        "ln_v_b": 0.02 * nrm(ks[7], (DEPTH, N_HEADS_A, HEAD_A)),
        "w_s": nrm(ks[8], (DEPTH, N_HEADS_A, MLP_CHUNK, MLP_CHUNK)) * MLP_CHUNK ** -0.5,
        "b_s": 1.0 + 0.02 * nrm(ks[9], (DEPTH, N_HEADS_A, MLP_CHUNK)),
        "pool_w": nrm(ks[10], (DEPTH, N_POOL_GROUPS, POOL_GROUP, POOL_GROUP)) * POOL_GROUP ** -0.5,
        "pool_scale": 1.0 + 0.02 * nrm(ks[11], (DEPTH, D_B)),
        "w_out": nrm(ks[12], (DEPTH, D_MIX, D_MODEL)) * (D_MIX ** -0.5) * BETA,
        "b_out": 0.01 * nrm(ks[13], (DEPTH, D_MODEL)),
        "ln_g": 1.0 + 0.02 * nrm(ks[14], (DEPTH, D_MODEL)),
        "ln_b": 0.02 * nrm(ks[15], (DEPTH, D_MODEL)),
        "w_ple": nrm(ks[16], (DEPTH, D_PLE, D_MODEL)) * D_PLE ** -0.5,
        "w_pg": nrm(ks[17], (DEPTH, D_MODEL, D_MODEL)) * D_MODEL ** -0.5,
        "b_pg": 0.01 * nrm(ks[18], (DEPTH, D_MODEL)),
    }


def reference(x_prompt, x_sample, state_pool, p_prompt, p_sample, w_in, ln_v_g, ln_v_b, w_s, b_s,
              pool_w, pool_scale, w_out, b_out, ln_g, ln_b, w_ple, w_pg, b_pg):
    yp, ys = x_prompt, x_sample
    hist_prompt = jnp.zeros((x_prompt.shape[0], POOL_HIST, D_B), x_prompt.dtype)
    pool_p, pool_s, v_s = [], [], []
    for i in range(DEPTH):
        prm = (w_in[i], ln_v_g[i], ln_v_b[i], w_s[i], b_s[i], pool_w[i], pool_scale[i],
               w_out[i], b_out[i], ln_g[i], ln_b[i], w_ple[i], w_pg[i], b_pg[i])
        yp, _, hp_new = _layer(yp, p_prompt[i], hist_prompt, 0, *prm)
        ys, v_new, hs_new = _layer(ys, p_sample[i], state_pool[i], PAST_LEN, *prm)
        pool_p.append(hp_new)
        pool_s.append(hs_new)
        v_s.append(v_new)
    state_pool_prompt = jnp.stack(pool_p)
    state_pool_sample = jnp.stack(pool_s)
    state_mlp_v_sample = jnp.stack(v_s)
    return (yp, ys, state_pool_prompt, state_pool_sample, state_mlp_v_sample)
```

```python
import functools
from typing import NamedTuple

import numpy as np
import jax
import jax.numpy as jnp
from jax.experimental import pallas as pl
from jax.experimental.pallas import tpu as pltpu

D_MODEL = 1024
DEPTH = 1
PAST_LEN = 4096
CHUNK = 64
MLP_CHUNK = 128
D_MIX = 2 * D_MODEL
D_A = D_MIX // 2
HEAD_A = 128
N_HEADS_A = D_A // HEAD_A
D_B = D_MIX - D_A
POOL_WINDOWS = (2, 4, 8, 16)
N_POOL_GROUPS = len(POOL_WINDOWS)
POOL_GROUP = D_B // N_POOL_GROUPS
POOL_HIST = max(POOL_WINDOWS) - 1
HIST_ROWS = POOL_HIST + 1
D_PLE = 256
D_IN = 3 * D_A + 2 * D_B
ALPHA = (2.0 * DEPTH) ** 0.25
LN_EPS = 1e-5

PROMPT_TILE = 256
SAMPLE_STREAMS_PER_TILE = 8
VMEM_LIMIT_BYTES = 56 * 1024 * 1024


class TileConfig(NamedTuple):
    n_sub: int
    sub_len: int
    first_pos_fix: bool
    emit_v: bool


def _layernorm(x, g, b):
    mu = jnp.mean(x, axis=-1, keepdims=True)
    xc = x - mu
    var = jnp.mean(xc * xc, axis=-1, keepdims=True)
    return xc * jax.lax.rsqrt(var + LN_EPS) * g + b


def _layer_kernel(cfg, x_ref, p_ref, hist_ref, cnt_ref, w_in_ref, lnv_g_ref, lnv_b_ref, ws_ref, bs_ref,
                  pool_w_ref, pool_scale_ref, w_out_ref, b_out_ref, ln_g_ref, ln_b_ref, w_ple_ref,
                  w_pg_ref, b_pg_ref, *out_and_scratch):
    if cfg.emit_v:
        y_ref, hist_out_ref, v_out_ref, z_ref, cat_ref = out_and_scratch
    else:
        y_ref, hist_out_ref, z_ref, cat_ref = out_and_scratch
    L = cfg.sub_len
    j = pl.program_id(1)

    x = x_ref[0]
    xb = x.astype(jnp.bfloat16)

    def proj(lo, hi):
        return jnp.dot(xb, w_in_ref[:, lo:hi], preferred_element_type=jnp.float32)

    v = proj(D_A, 2 * D_A)
    for h in range(N_HEADS_A):
        sl = slice(h * HEAD_A, (h + 1) * HEAD_A)
        vn = _layernorm(v[:, sl], lnv_g_ref[:, sl], lnv_b_ref[:, sl])
        if cfg.emit_v:
            v_out_ref[0, :, sl] = vn
        vnb = vn.astype(jnp.bfloat16)
        for s in range(cfg.n_sub):
            for c0 in range(0, L, MLP_CHUNK):
                lc = min(MLP_CHUNK, L - c0)
                r0 = s * L + c0
                mixed = jnp.dot(ws_ref[h, :lc, :lc], vnb[r0:r0 + lc], preferred_element_type=jnp.float32)
                cat_ref[r0:r0 + lc, sl] = (mixed + bs_ref[:lc, sl]).astype(jnp.bfloat16)
    u = proj(0, D_A)
    ga = proj(2 * D_A, 3 * D_A)
    a_out = u * cat_ref[:, :D_A].astype(jnp.float32) * jax.nn.silu(ga)
    cat_ref[:, :D_A] = a_out.astype(jnp.bfloat16)

    hp = proj(3 * D_A, 3 * D_A + D_B)
    gb = proj(3 * D_A + D_B, D_IN)

    @pl.when(j == 0)
    def _():
        z_ref[:, :HIST_ROWS, :] = hist_ref[...]

    for s in range(cfg.n_sub):
        z_ref[s, HIST_ROWS:, :] = hp[s * L:(s + 1) * L]
    for s in range(cfg.n_sub):
        rows = slice(s * L, (s + 1) * L)
        for g, w in enumerate(POOL_WINDOWS):
            cols = slice(g * POOL_GROUP, (g + 1) * POOL_GROUP)
            acc = z_ref[s, :, cols]
            span = 1
            while span < w:
                acc = acc + pltpu.roll(acc, span, axis=0)
                span *= 2
            win = acc[HIST_ROWS:]
            hp_g = hp[rows, cols]
            pooled = win * (1.0 / w) - hp_g
            cat_ref[rows, D_A + g * POOL_GROUP:D_A + (g + 1) * POOL_GROUP] = pooled.astype(jnp.bfloat16)
            if cfg.first_pos_fix:
                cnt = jnp.where(j == 0, cnt_ref[:, cols], float(w))
                head = win[:HIST_ROWS] / cnt - hp_g[:HIST_ROWS]
                cat_ref[s * L:s * L + HIST_ROWS, D_A + g * POOL_GROUP:D_A + (g + 1) * POOL_GROUP] = (
                    head.astype(jnp.bfloat16))
        last = z_ref[s, L:, :]
        hist_out_ref[s] = last
        z_ref[s, :HIST_ROWS, :] = last
    for g in range(N_POOL_GROUPS):
        cols = slice(g * POOL_GROUP, (g + 1) * POOL_GROUP)
        ccols = slice(D_A + g * POOL_GROUP, D_A + (g + 1) * POOL_GROUP)
        q = jnp.dot(cat_ref[:, ccols], pool_w_ref[g], preferred_element_type=jnp.float32)
        cat_ref[:, ccols] = (q * pool_scale_ref[:, cols] * jax.nn.silu(gb[:, cols])).astype(jnp.bfloat16)

    y = jnp.dot(cat_ref[...], w_out_ref[...], preferred_element_type=jnp.float32) + b_out_ref[...]
    xn = _layernorm(ALPHA * x + y, ln_g_ref[...], ln_b_ref[...])
    gate = jax.nn.sigmoid(
        jnp.dot(xn.astype(jnp.bfloat16), w_pg_ref[...], preferred_element_type=jnp.float32) + b_pg_ref[...])
    ple = jnp.dot(p_ref[0].astype(jnp.bfloat16), w_ple_ref[...], preferred_element_type=jnp.float32)
    y_ref[0] = xn + gate * ple


def _resident(shape):
    nd = len(shape)
    return pl.BlockSpec(shape, lambda i, j: (0,) * nd, pipeline_mode=pl.Buffered(1))


def _run_layer(cfg, n_outer, n_inner, x, p, hist, cnt, weights):
    T = cfg.n_sub * cfg.sub_len
    assert x.shape == (n_outer, n_inner * T, D_MODEL) and p.shape == (n_outer, n_inner * T, D_PLE)
    assert hist.shape == (n_outer * cfg.n_sub, HIST_ROWS, D_B)
    assert n_inner == 1 or cfg.n_sub == 1
    in_specs = [
        pl.BlockSpec((1, T, D_MODEL), lambda i, j: (i, j, 0)),
        pl.BlockSpec((1, T, D_PLE), lambda i, j: (i, j, 0)),
        pl.BlockSpec((cfg.n_sub, HIST_ROWS, D_B), lambda i, j: (i, 0, 0)),
        _resident(cnt.shape),
    ] + [_resident(w.shape) for w in weights]
    out_shape = [jax.ShapeDtypeStruct(x.shape, jnp.float32),
                 jax.ShapeDtypeStruct(hist.shape, jnp.float32)]
    out_specs = [pl.BlockSpec((1, T, D_MODEL), lambda i, j: (i, j, 0)),
                 pl.BlockSpec((cfg.n_sub, HIST_ROWS, D_B), lambda i, j: (i, 0, 0))]
    if cfg.emit_v:
        out_shape.append(jax.ShapeDtypeStruct(x.shape[:2] + (D_A,), jnp.float32))
        out_specs.append(pl.BlockSpec((1, T, D_A), lambda i, j: (i, j, 0)))
    return pl.pallas_call(
        functools.partial(_layer_kernel, cfg),
        grid=(n_outer, n_inner),
        in_specs=in_specs,
        out_specs=out_specs,
        out_shape=out_shape,
        scratch_shapes=[pltpu.VMEM((cfg.n_sub, HIST_ROWS + cfg.sub_len, D_B), jnp.float32),
                        pltpu.VMEM((T, D_MIX), jnp.bfloat16)],
        compiler_params=pltpu.CompilerParams(dimension_semantics=("arbitrary", "arbitrary"),
                                             vmem_limit_bytes=VMEM_LIMIT_BYTES),
        name="layer_sample" if cfg.emit_v else "layer_prompt",
    )(x, p, hist, cnt, *weights)


def kernel(x_prompt, x_sample, state_pool, p_prompt, p_sample, w_in, ln_v_g, ln_v_b, w_s, b_s, pool_w, pool_scale, w_out, b_out, ln_g, ln_b, w_ple, w_pg, b_pg):
    assert DEPTH == 1 and w_in.shape[0] == DEPTH
    B, S, _ = x_prompt.shape
    Bs, Ss, _ = x_sample.shape
    bf = jnp.bfloat16
    row = lambda a: a.reshape(1, -1).astype(jnp.float32)

    blk = np.arange(MLP_CHUNK) // CHUNK
    mask = jnp.asarray(blk[None, :] <= blk[:, None])
    ws_masked = jnp.where(mask[None], w_s[0], jnp.zeros((), w_s.dtype)).astype(bf)
    bs_tbl = jnp.repeat(jnp.transpose(b_s[0]), HEAD_A, axis=1).astype(jnp.float32)
    win = np.repeat(np.asarray(POOL_WINDOWS, np.float32), POOL_GROUP)[None, :]
    cnt = jnp.asarray(np.minimum(np.arange(1, HIST_ROWS + 1, dtype=np.float32)[:, None], win))
    weights = (w_in[0].astype(bf), row(ln_v_g[0]), row(ln_v_b[0]), ws_masked, bs_tbl, pool_w[0].astype(bf),
               row(pool_scale[0]), w_out[0].astype(bf), row(b_out[0]), row(ln_g[0]), row(ln_b[0]),
               w_ple[0].astype(bf), w_pg[0].astype(bf), row(b_pg[0]))

    cfg_p = TileConfig(n_sub=1, sub_len=PROMPT_TILE, first_pos_fix=True, emit_v=False)
    hist_p = jnp.zeros((B, HIST_ROWS, D_B), jnp.float32)
    y_p, hist_p_new = _run_layer(cfg_p, B, S // PROMPT_TILE, x_prompt, p_prompt[0], hist_p, cnt, weights)

    assert PAST_LEN >= POOL_HIST
    ns = SAMPLE_STREAMS_PER_TILE
    cfg_s = TileConfig(n_sub=ns, sub_len=Ss, first_pos_fix=False, emit_v=True)
    hist_s = jnp.pad(state_pool[0], ((0, 0), (1, 0), (0, 0)))
    y_s, hist_s_new, v_s = _run_layer(
        cfg_s, Bs // ns, 1, x_sample.reshape(Bs // ns, ns * Ss, D_MODEL),
        p_sample[0].reshape(Bs // ns, ns * Ss, D_PLE), hist_s, cnt, weights)

    return (y_p,
            y_s.reshape(Bs, Ss, D_MODEL),
            hist_p_new[None, :, 1:, :],
            hist_s_new[None, :, 1:, :],
            v_s.reshape(1, Bs, Ss, N_HEADS_A, HEAD_A))
```

```python
import functools
from typing import NamedTuple

import numpy as np
import jax
import jax.numpy as jnp
from jax.experimental import pallas as pl
from jax.experimental.pallas import tpu as pltpu

D_MODEL = 1024
DEPTH = 1
PAST_LEN = 4096
CHUNK = 64
MLP_CHUNK = 128
D_MIX = 2 * D_MODEL
D_A = D_MIX // 2
HEAD_A = 128
N_HEADS_A = D_A // HEAD_A
D_B = D_MIX - D_A
POOL_WINDOWS = (2, 4, 8, 16)
N_POOL_GROUPS = len(POOL_WINDOWS)
POOL_GROUP = D_B // N_POOL_GROUPS
POOL_HIST = max(POOL_WINDOWS) - 1
HIST_ROWS = POOL_HIST + 1
D_PLE = 256
D_IN = 3 * D_A + 2 * D_B
ALPHA = (2.0 * DEPTH) ** 0.25
LN_EPS = 1e-5

PROMPT_BLOCK_ROWS = 256
PROMPT_BLOCKS_PER_TILE = 2
SAMPLE_STREAMS_PER_TILE = 8
VMEM_LIMIT_BYTES = 56 * 1024 * 1024


class TileConfig(NamedTuple):
    n_chain: int
    n_sub: int
    sub_len: int
    sequential: bool
    emit_v: bool

    @property
    def chain_rows(self):
        return self.n_sub * self.sub_len

    @property
    def tile_rows(self):
        return self.n_chain * self.chain_rows

    @property
    def n_pieces(self):
        return self.n_chain * self.n_sub


def _layernorm(x, g, b):
    mu = jnp.mean(x, axis=-1, keepdims=True)
    xc = x - mu
    var = jnp.mean(xc * xc, axis=-1, keepdims=True)
    return xc * jax.lax.rsqrt(var + LN_EPS) * g + b


def _layer_kernel(cfg, x_ref, p_ref, hist_ref, cnt_ref, w_in_ref, lnv_g_ref, lnv_b_ref, ws_ref, bs_ref,
                  pool_w_ref, pool_scale_ref, w_out_ref, b_out_ref, ln_g_ref, ln_b_ref, w_ple_ref,
                  w_pg_ref, b_pg_ref, *out_and_scratch):
    if cfg.emit_v:
        y_ref, hist_out_ref, v_out_ref, z_ref, cat_ref = out_and_scratch
    else:
        y_ref, hist_out_ref, z_ref, cat_ref = out_and_scratch
        v_out_ref = None
    L = cfg.sub_len
    j = pl.program_id(1)
    last_piece = cfg.n_pieces - 1

    if cfg.sequential:
        @pl.when(j == 0)
        def _():
            z_ref[last_piece, L:, :] = hist_ref[0]

        z_ref[0, :HIST_ROWS, :] = z_ref[last_piece, L:, :]
    else:
        z_ref[:, :HIST_ROWS, :] = hist_ref[...]

    for c in range(cfg.n_chain):
        _layer_rows(cfg, c, j, x_ref, p_ref, cnt_ref, w_in_ref, lnv_g_ref, lnv_b_ref, ws_ref, bs_ref,
                    pool_w_ref, pool_scale_ref, w_out_ref, b_out_ref, ln_g_ref, ln_b_ref, w_ple_ref,
                    w_pg_ref, b_pg_ref, y_ref, v_out_ref, z_ref, cat_ref)

    if cfg.sequential:
        hist_out_ref[0] = z_ref[last_piece, L:, :]
    else:
        for k in range(cfg.n_pieces):
            hist_out_ref[k] = z_ref[k, L:, :]


def _layer_rows(cfg, c, j, x_ref, p_ref, cnt_ref, w_in_ref, lnv_g_ref, lnv_b_ref, ws_ref, bs_ref,
                pool_w_ref, pool_scale_ref, w_out_ref, b_out_ref, ln_g_ref, ln_b_ref, w_ple_ref,
                w_pg_ref, b_pg_ref, y_ref, v_out_ref, z_ref, cat_ref):
    L = cfg.sub_len
    R = cfg.chain_rows
    blk = slice(c * R, (c + 1) * R)
    x = x_ref[0, blk, :]
    xb = x.astype(jnp.bfloat16)

    def proj(lo, hi):
        return jnp.dot(xb, w_in_ref[:, lo:hi], preferred_element_type=jnp.float32)

    v = proj(D_A, 2 * D_A)
    for h in range(N_HEADS_A):
        sl = slice(h * HEAD_A, (h + 1) * HEAD_A)
        vn = _layernorm(v[:, sl], lnv_g_ref[:, sl], lnv_b_ref[:, sl])
        if v_out_ref is not None:
            v_out_ref[0, blk, sl] = vn
        vnb = vn.astype(jnp.bfloat16)
        for s in range(cfg.n_sub):
            for c0 in range(0, L, MLP_CHUNK):
                lc = min(MLP_CHUNK, L - c0)
                r0 = s * L + c0
                mixed = jnp.dot(ws_ref[h, :lc, :lc], vnb[r0:r0 + lc], preferred_element_type=jnp.float32)
                cat_ref[c, r0:r0 + lc, sl] = (mixed + bs_ref[:lc, sl]).astype(jnp.bfloat16)
    u = proj(0, D_A)
    ga = proj(2 * D_A, 3 * D_A)
    a_out = u * cat_ref[c, :, :D_A].astype(jnp.float32) * jax.nn.silu(ga)
    cat_ref[c, :, :D_A] = a_out.astype(jnp.bfloat16)

    hp = proj(3 * D_A, 3 * D_A + D_B)
    gb = proj(3 * D_A + D_B, D_IN)
    for s in range(cfg.n_sub):
        k = c * cfg.n_sub + s
        rows = slice(s * L, (s + 1) * L)
        if cfg.sequential and k > 0:
            z_ref[k, :HIST_ROWS, :] = z_ref[k - 1, L:, :]
        z_ref[k, HIST_ROWS:, :] = hp[rows]
        for g, w in enumerate(POOL_WINDOWS):
            cols = slice(g * POOL_GROUP, (g + 1) * POOL_GROUP)
            ccols = slice(D_A + g * POOL_GROUP, D_A + (g + 1) * POOL_GROUP)
            acc = z_ref[k, :, cols]
            span = 1
            while span < w:
                acc = acc + pltpu.roll(acc, span, axis=0)
                span *= 2
            win = acc[HIST_ROWS:]
            hp_g = hp[rows, cols]
            cat_ref[c, rows, ccols] = (win * (1.0 / w) - hp_g).astype(jnp.bfloat16)
            if cfg.sequential and k == 0:
                cnt = jnp.where(j == 0, cnt_ref[:, cols], float(w))
                head = win[:HIST_ROWS] / cnt - hp_g[:HIST_ROWS]
                cat_ref[c, :HIST_ROWS, ccols] = head.astype(jnp.bfloat16)
    for g in range(N_POOL_GROUPS):
        cols = slice(g * POOL_GROUP, (g + 1) * POOL_GROUP)
        ccols = slice(D_A + g * POOL_GROUP, D_A + (g + 1) * POOL_GROUP)
        q = jnp.dot(cat_ref[c, :, ccols], pool_w_ref[g], preferred_element_type=jnp.float32)
        cat_ref[c, :, ccols] = (q * pool_scale_ref[:, cols] * jax.nn.silu(gb[:, cols])).astype(jnp.bfloat16)

    y = jnp.dot(cat_ref[c], w_out_ref[...], preferred_element_type=jnp.float32) + b_out_ref[...]
    xn = _layernorm(ALPHA * x + y, ln_g_ref[...], ln_b_ref[...])
    gate = jax.nn.sigmoid(
        jnp.dot(xn.astype(jnp.bfloat16), w_pg_ref[...], preferred_element_type=jnp.float32) + b_pg_ref[...])
    ple = jnp.dot(p_ref[0, blk, :].astype(jnp.bfloat16), w_ple_ref[...], preferred_element_type=jnp.float32)
    y_ref[0, blk, :] = xn + gate * ple


def _resident(shape):
    nd = len(shape)
    return pl.BlockSpec(shape, lambda i, j: (0,) * nd, pipeline_mode=pl.Buffered(1))


def _run_layer(cfg, n_outer, n_inner, x, p, hist, cnt, weights):
    T = cfg.tile_rows
    n_hist = 1 if cfg.sequential else cfg.n_pieces
    assert x.shape == (n_outer, n_inner * T, D_MODEL) and p.shape == (n_outer, n_inner * T, D_PLE)
    assert hist.shape == (n_outer * n_hist, HIST_ROWS, D_B)
    assert cfg.sequential or n_inner == 1
    assert not cfg.sequential or cfg.n_sub == 1
    in_specs = [
        pl.BlockSpec((1, T, D_MODEL), lambda i, j: (i, j, 0)),
        pl.BlockSpec((1, T, D_PLE), lambda i, j: (i, j, 0)),
        pl.BlockSpec((n_hist, HIST_ROWS, D_B), lambda i, j: (i, 0, 0)),
        _resident(cnt.shape),
    ] + [_resident(w.shape) for w in weights]
    out_shape = [jax.ShapeDtypeStruct(x.shape, jnp.float32),
                 jax.ShapeDtypeStruct(hist.shape, jnp.float32)]
    out_specs = [pl.BlockSpec((1, T, D_MODEL), lambda i, j: (i, j, 0)),
                 pl.BlockSpec((n_hist, HIST_ROWS, D_B), lambda i, j: (i, 0, 0))]
    if cfg.emit_v:
        out_shape.append(jax.ShapeDtypeStruct(x.shape[:2] + (D_A,), jnp.float32))
        out_specs.append(pl.BlockSpec((1, T, D_A), lambda i, j: (i, j, 0)))
    return pl.pallas_call(
        functools.partial(_layer_kernel, cfg),
        grid=(n_outer, n_inner),
        in_specs=in_specs,
        out_specs=out_specs,
        out_shape=out_shape,
        scratch_shapes=[pltpu.VMEM((cfg.n_pieces, HIST_ROWS + cfg.sub_len, D_B), jnp.float32),
                        pltpu.VMEM((cfg.n_chain, cfg.chain_rows, D_MIX), jnp.bfloat16)],
        compiler_params=pltpu.CompilerParams(dimension_semantics=("arbitrary", "arbitrary"),
                                             vmem_limit_bytes=VMEM_LIMIT_BYTES),
        name="layer_sample" if cfg.emit_v else "layer_prompt",
    )(x, p, hist, cnt, *weights)


def kernel(x_prompt, x_sample, state_pool, p_prompt, p_sample, w_in, ln_v_g, ln_v_b, w_s, b_s, pool_w, pool_scale, w_out, b_out, ln_g, ln_b, w_ple, w_pg, b_pg):
    assert DEPTH == 1 and w_in.shape[0] == DEPTH
    B, S, _ = x_prompt.shape
    Bs, Ss, _ = x_sample.shape
    bf = jnp.bfloat16
    row = lambda a: a.reshape(1, -1).astype(jnp.float32)

    blk = np.arange(MLP_CHUNK) // CHUNK
    mask = jnp.asarray(blk[None, :] <= blk[:, None])
    ws_masked = jnp.where(mask[None], w_s[0], jnp.zeros((), w_s.dtype)).astype(bf)
    bs_tbl = jnp.repeat(jnp.transpose(b_s[0]), HEAD_A, axis=1).astype(jnp.float32)
    win = np.repeat(np.asarray(POOL_WINDOWS, np.float32), POOL_GROUP)[None, :]
    cnt = jnp.asarray(np.minimum(np.arange(1, HIST_ROWS + 1, dtype=np.float32)[:, None], win))
    weights = (w_in[0].astype(bf), row(ln_v_g[0]), row(ln_v_b[0]), ws_masked, bs_tbl, pool_w[0].astype(bf),
               row(pool_scale[0]), w_out[0].astype(bf), row(b_out[0]), row(ln_g[0]), row(ln_b[0]),
               w_ple[0].astype(bf), w_pg[0].astype(bf), row(b_pg[0]))

    cfg_p = TileConfig(n_chain=PROMPT_BLOCKS_PER_TILE, n_sub=1, sub_len=PROMPT_BLOCK_ROWS,
                       sequential=True, emit_v=False)
    hist_p = jnp.zeros((B, HIST_ROWS, D_B), jnp.float32)
    y_p, hist_p_new = _run_layer(cfg_p, B, S // cfg_p.tile_rows, x_prompt, p_prompt[0], hist_p, cnt, weights)

    assert PAST_LEN >= POOL_HIST
    ns = SAMPLE_STREAMS_PER_TILE
    cfg_s = TileConfig(n_chain=1, n_sub=ns, sub_len=Ss, sequential=False, emit_v=True)
    hist_s = jnp.pad(state_pool[0], ((0, 0), (1, 0), (0, 0)))
    y_s, hist_s_new, v_s = _run_layer(
        cfg_s, Bs // ns, 1, x_sample.reshape(Bs // ns, ns * Ss, D_MODEL),
        p_sample[0].reshape(Bs // ns, ns * Ss, D_PLE), hist_s, cnt, weights)

    return (y_p,
            y_s.reshape(Bs, Ss, D_MODEL),
            hist_p_new[None, :, 1:, :],
            hist_s_new[None, :, 1:, :],
            v_s.reshape(1, Bs, Ss, N_HEADS_A, HEAD_A))
```

```python
import functools
from typing import NamedTuple

import numpy as np
import jax
import jax.numpy as jnp
from jax.experimental import pallas as pl
from jax.experimental.pallas import tpu as pltpu

D_MODEL = 1024
DEPTH = 1
PAST_LEN = 4096
CHUNK = 64
MLP_CHUNK = 128
D_MIX = 2 * D_MODEL
D_A = D_MIX // 2
HEAD_A = 128
N_HEADS_A = D_A // HEAD_A
D_B = D_MIX - D_A
POOL_WINDOWS = (2, 4, 8, 16)
N_POOL_GROUPS = len(POOL_WINDOWS)
POOL_GROUP = D_B // N_POOL_GROUPS
POOL_HIST = max(POOL_WINDOWS) - 1
HIST_ROWS = POOL_HIST + 1
D_PLE = 256
D_IN = 3 * D_A + 2 * D_B
ALPHA = (2.0 * DEPTH) ** 0.25
LN_EPS = 1e-5

PROMPT_BLOCK_ROWS = 512
PROMPT_BLOCKS_PER_TILE = 1
SAMPLE_STREAMS_PER_TILE = 8
VMEM_LIMIT_BYTES = 56 * 1024 * 1024


class TileConfig(NamedTuple):
    n_chain: int
    n_sub: int
    sub_len: int
    sequential: bool
    emit_v: bool

    @property
    def chain_rows(self):
        return self.n_sub * self.sub_len

    @property
    def tile_rows(self):
        return self.n_chain * self.chain_rows

    @property
    def n_pieces(self):
        return self.n_chain * self.n_sub


def _layernorm(x, g, b):
    mu = jnp.mean(x, axis=-1, keepdims=True)
    xc = x - mu
    var = jnp.mean(xc * xc, axis=-1, keepdims=True)
    return xc * jax.lax.rsqrt(var + LN_EPS) * g + b


def _layer_kernel(cfg, x_ref, p_ref, hist_ref, cnt_ref, w_in_ref, lnv_g_ref, lnv_b_ref, ws_ref, bs_ref,
                  pool_w_ref, pool_scale_ref, w_out_ref, b_out_ref, ln_g_ref, ln_b_ref, w_ple_ref,
                  w_pg_ref, b_pg_ref, *out_and_scratch):
    if cfg.emit_v:
        y_ref, hist_out_ref, v_out_ref, z_ref, cat_ref = out_and_scratch
    else:
        y_ref, hist_out_ref, z_ref, cat_ref = out_and_scratch
        v_out_ref = None
    L = cfg.sub_len
    j = pl.program_id(1)
    last_piece = cfg.n_pieces - 1

    if cfg.sequential:
        @pl.when(j == 0)
        def _():
            z_ref[last_piece, L:, :] = hist_ref[0]

        z_ref[0, :HIST_ROWS, :] = z_ref[last_piece, L:, :]
    else:
        z_ref[:, :HIST_ROWS, :] = hist_ref[...]

    for c in range(cfg.n_chain):
        _layer_rows(cfg, c, j, x_ref, p_ref, cnt_ref, w_in_ref, lnv_g_ref, lnv_b_ref, ws_ref, bs_ref,
                    pool_w_ref, pool_scale_ref, w_out_ref, b_out_ref, ln_g_ref, ln_b_ref, w_ple_ref,
                    w_pg_ref, b_pg_ref, y_ref, v_out_ref, z_ref, cat_ref)

    if cfg.sequential:
        hist_out_ref[0] = z_ref[last_piece, L:, :]
    else:
        for k in range(cfg.n_pieces):
            hist_out_ref[k] = z_ref[k, L:, :]


def _layer_rows(cfg, c, j, x_ref, p_ref, cnt_ref, w_in_ref, lnv_g_ref, lnv_b_ref, ws_ref, bs_ref,
                pool_w_ref, pool_scale_ref, w_out_ref, b_out_ref, ln_g_ref, ln_b_ref, w_ple_ref,
                w_pg_ref, b_pg_ref, y_ref, v_out_ref, z_ref, cat_ref):
    L = cfg.sub_len
    R = cfg.chain_rows
    blk = slice(c * R, (c + 1) * R)
    x = x_ref[0, blk, :]
    xb = x.astype(jnp.bfloat16)

    def proj(lo, hi):
        return jnp.dot(xb, w_in_ref[:, lo:hi], preferred_element_type=jnp.float32)

    v = proj(D_A, 2 * D_A)
    for h in range(N_HEADS_A):
        sl = slice(h * HEAD_A, (h + 1) * HEAD_A)
        vn = _layernorm(v[:, sl], lnv_g_ref[:, sl], lnv_b_ref[:, sl])
        if v_out_ref is not None:
            v_out_ref[0, blk, sl] = vn
        vnb = vn.astype(jnp.bfloat16)
        for s in range(cfg.n_sub):
            for c0 in range(0, L, MLP_CHUNK):
                lc = min(MLP_CHUNK, L - c0)
                r0 = s * L + c0
                mixed = jnp.dot(ws_ref[h, :lc, :lc], vnb[r0:r0 + lc], preferred_element_type=jnp.float32)
                cat_ref[c, r0:r0 + lc, sl] = (mixed + bs_ref[:lc, sl]).astype(jnp.bfloat16)
    u = proj(0, D_A)
    ga = proj(2 * D_A, 3 * D_A)
    a_out = u * cat_ref[c, :, :D_A].astype(jnp.float32) * jax.nn.silu(ga)
    cat_ref[c, :, :D_A] = a_out.astype(jnp.bfloat16)

    hp = proj(3 * D_A, 3 * D_A + D_B)
    gb = proj(3 * D_A + D_B, D_IN)
    for s in range(cfg.n_sub):
        k = c * cfg.n_sub + s
        rows = slice(s * L, (s + 1) * L)
        if cfg.sequential and k > 0:
            z_ref[k, :HIST_ROWS, :] = z_ref[k - 1, L:, :]
        z_ref[k, HIST_ROWS:, :] = hp[rows]
        for g, w in enumerate(POOL_WINDOWS):
            cols = slice(g * POOL_GROUP, (g + 1) * POOL_GROUP)
            ccols = slice(D_A + g * POOL_GROUP, D_A + (g + 1) * POOL_GROUP)
            acc = z_ref[k, :, cols]
            span = 1
            while span < w:
                acc = acc + pltpu.roll(acc, span, axis=0)
                span *= 2
            win = acc[HIST_ROWS:]
            hp_g = hp[rows, cols]
            cat_ref[c, rows, ccols] = (win * (1.0 / w) - hp_g).astype(jnp.bfloat16)
            if cfg.sequential and k == 0:
                cnt = jnp.where(j == 0, cnt_ref[:, cols], float(w))
                head = win[:HIST_ROWS] / cnt - hp_g[:HIST_ROWS]
                cat_ref[c, :HIST_ROWS, ccols] = head.astype(jnp.bfloat16)
    for g in range(N_POOL_GROUPS):
        cols = slice(g * POOL_GROUP, (g + 1) * POOL_GROUP)
        ccols = slice(D_A + g * POOL_GROUP, D_A + (g + 1) * POOL_GROUP)
        q = jnp.dot(cat_ref[c, :, ccols], pool_w_ref[g], preferred_element_type=jnp.float32)
        cat_ref[c, :, ccols] = (q * pool_scale_ref[:, cols] * jax.nn.silu(gb[:, cols])).astype(jnp.bfloat16)

    y = jnp.dot(cat_ref[c], w_out_ref[...], preferred_element_type=jnp.float32) + b_out_ref[...]
    xn = _layernorm(ALPHA * x + y, ln_g_ref[...], ln_b_ref[...])
    gate = jax.nn.sigmoid(
        jnp.dot(xn.astype(jnp.bfloat16), w_pg_ref[...], preferred_element_type=jnp.float32) + b_pg_ref[...])
    ple = jnp.dot(p_ref[0, blk, :].astype(jnp.bfloat16), w_ple_ref[...], preferred_element_type=jnp.float32)
    y_ref[0, blk, :] = xn + gate * ple


def _resident(shape):
    nd = len(shape)
    return pl.BlockSpec(shape, lambda i, j: (0,) * nd, pipeline_mode=pl.Buffered(1))


def _run_layer(cfg, n_outer, n_inner, x, p, hist, cnt, weights):
    T = cfg.tile_rows
    n_hist = 1 if cfg.sequential else cfg.n_pieces
    assert x.shape == (n_outer, n_inner * T, D_MODEL) and p.shape == (n_outer, n_inner * T, D_PLE)
    assert hist.shape == (n_outer * n_hist, HIST_ROWS, D_B)
    assert cfg.sequential or n_inner == 1
    assert not cfg.sequential or cfg.n_sub == 1
    in_specs = [
        pl.BlockSpec((1, T, D_MODEL), lambda i, j: (i, j, 0)),
        pl.BlockSpec((1, T, D_PLE), lambda i, j: (i, j, 0)),
        pl.BlockSpec((n_hist, HIST_ROWS, D_B), lambda i, j: (i, 0, 0)),
        _resident(cnt.shape),
    ] + [_resident(w.shape) for w in weights]
    out_shape = [jax.ShapeDtypeStruct(x.shape, jnp.float32),
                 jax.ShapeDtypeStruct(hist.shape, jnp.float32)]
    out_specs = [pl.BlockSpec((1, T, D_MODEL), lambda i, j: (i, j, 0)),
                 pl.BlockSpec((n_hist, HIST_ROWS, D_B), lambda i, j: (i, 0, 0))]
    if cfg.emit_v:
        out_shape.append(jax.ShapeDtypeStruct(x.shape[:2] + (D_A,), jnp.float32))
        out_specs.append(pl.BlockSpec((1, T, D_A), lambda i, j: (i, j, 0)))
    return pl.pallas_call(
        functools.partial(_layer_kernel, cfg),
        grid=(n_outer, n_inner),
        in_specs=in_specs,
        out_specs=out_specs,
        out_shape=out_shape,
        scratch_shapes=[pltpu.VMEM((cfg.n_pieces, HIST_ROWS + cfg.sub_len, D_B), jnp.float32),
                        pltpu.VMEM((cfg.n_chain, cfg.chain_rows, D_MIX), jnp.bfloat16)],
        compiler_params=pltpu.CompilerParams(dimension_semantics=("arbitrary", "arbitrary"),
                                             vmem_limit_bytes=VMEM_LIMIT_BYTES),
        name="layer_sample" if cfg.emit_v else "layer_prompt",
    )(x, p, hist, cnt, *weights)


def kernel(x_prompt, x_sample, state_pool, p_prompt, p_sample, w_in, ln_v_g, ln_v_b, w_s, b_s, pool_w, pool_scale, w_out, b_out, ln_g, ln_b, w_ple, w_pg, b_pg):
    assert DEPTH == 1 and w_in.shape[0] == DEPTH
    B, S, _ = x_prompt.shape
    Bs, Ss, _ = x_sample.shape
    bf = jnp.bfloat16
    row = lambda a: a.reshape(1, -1).astype(jnp.float32)

    blk = np.arange(MLP_CHUNK) // CHUNK
    mask = jnp.asarray(blk[None, :] <= blk[:, None])
    ws_masked = jnp.where(mask[None], w_s[0], jnp.zeros((), w_s.dtype)).astype(bf)
    bs_tbl = jnp.repeat(jnp.transpose(b_s[0]), HEAD_A, axis=1).astype(jnp.float32)
    win = np.repeat(np.asarray(POOL_WINDOWS, np.float32), POOL_GROUP)[None, :]
    cnt = jnp.asarray(np.minimum(np.arange(1, HIST_ROWS + 1, dtype=np.float32)[:, None], win))
    weights = (w_in[0].astype(bf), row(ln_v_g[0]), row(ln_v_b[0]), ws_masked, bs_tbl, pool_w[0].astype(bf),
               row(pool_scale[0]), w_out[0].astype(bf), row(b_out[0]), row(ln_g[0]), row(ln_b[0]),
               w_ple[0].astype(bf), w_pg[0].astype(bf), row(b_pg[0]))

    cfg_p = TileConfig(n_chain=PROMPT_BLOCKS_PER_TILE, n_sub=1, sub_len=PROMPT_BLOCK_ROWS,
                       sequential=True, emit_v=False)
    hist_p = jnp.zeros((B, HIST_ROWS, D_B), jnp.float32)
    y_p, hist_p_new = _run_layer(cfg_p, B, S // cfg_p.tile_rows, x_prompt, p_prompt[0], hist_p, cnt, weights)

    assert PAST_LEN >= POOL_HIST
    ns = SAMPLE_STREAMS_PER_TILE
    cfg_s = TileConfig(n_chain=1, n_sub=ns, sub_len=Ss, sequential=False, emit_v=True)
    hist_s = jnp.pad(state_pool[0], ((0, 0), (1, 0), (0, 0)))
    y_s, hist_s_new, v_s = _run_layer(
        cfg_s, Bs // ns, 1, x_sample.reshape(Bs // ns, ns * Ss, D_MODEL),
        p_sample[0].reshape(Bs // ns, ns * Ss, D_PLE), hist_s, cnt, weights)

    return (y_p,
            y_s.reshape(Bs, Ss, D_MODEL),
            hist_p_new[None, :, 1:, :],
            hist_s_new[None, :, 1:, :],
            v_s.reshape(1, Bs, Ss, N_HEADS_A, HEAD_A))
```

```python
import functools
from typing import NamedTuple

import numpy as np
import jax
import jax.numpy as jnp
from jax.experimental import pallas as pl
from jax.experimental.pallas import tpu as pltpu

D_MODEL = 1024
DEPTH = 1
PAST_LEN = 4096
CHUNK = 64
MLP_CHUNK = 128
D_MIX = 2 * D_MODEL
D_A = D_MIX // 2
HEAD_A = 128
N_HEADS_A = D_A // HEAD_A
D_B = D_MIX - D_A
POOL_WINDOWS = (2, 4, 8, 16)
N_POOL_GROUPS = len(POOL_WINDOWS)
POOL_GROUP = D_B // N_POOL_GROUPS
POOL_HIST = max(POOL_WINDOWS) - 1
HIST_ROWS = POOL_HIST + 1
D_PLE = 256
D_IN = 3 * D_A + 2 * D_B
ALPHA = (2.0 * DEPTH) ** 0.25
LN_EPS = 1e-5

PROMPT_TILE_ROWS = 512
SAMPLE_STREAMS_PER_TILE = 8
VMEM_LIMIT_BYTES = 56 * 1024 * 1024
COL_PIECE = 256
N_COL_PIECES = D_MODEL // COL_PIECE
assert COL_PIECE == POOL_GROUP and COL_PIECE == 2 * HEAD_A


def _numbered(name, n):
    return tuple(f"{name}{i}" for i in range(n))


STAGE1_PIECES = (("v_proj", "u_proj") + _numbered("ga_proj", N_COL_PIECES) + _numbered("ln_v", N_HEADS_A)
                 + _numbered("a_out", N_COL_PIECES) + ("hp_proj", "gb_proj") + _numbered("pooling", N_POOL_GROUPS)
                 + _numbered("b_out", N_POOL_GROUPS))
STAGE2_PIECES = _numbered("w_out", N_COL_PIECES) + ("post_norm",) + _numbered("gate", N_COL_PIECES)
PIPELINE_ORDER = (
    "v_proj", "hp_proj", "w_out0", "w_out1", "w_out2", "w_out3", "pooling0", "pooling1", "pooling2", "pooling3",
    "u_proj",
    "ln_v0", "ln_v1", "ga_proj0", "ln_v2", "ln_v3", "ga_proj1",
    "ln_v4", "ln_v5", "ga_proj2", "ln_v6", "ln_v7", "ga_proj3",
    "post_norm", "a_out0", "a_out1", "a_out2", "a_out3",
    "gate0", "gate1", "gate2", "gate3", "gb_proj",
    "b_out0", "b_out1", "b_out2", "b_out3")
assert sorted(PIPELINE_ORDER) == sorted(STAGE1_PIECES + STAGE2_PIECES)


class TileConfig(NamedTuple):
    n_sub: int
    sub_len: int
    n_inner: int
    n_tiles: int
    sequential: bool
    pipelined: bool
    emit_v: bool

    @property
    def tile_rows(self):
        return self.n_sub * self.sub_len


def _layernorm(x, g, b):
    mu = jnp.mean(x, axis=-1, keepdims=True)
    xc = x - mu
    var = jnp.mean(xc * xc, axis=-1, keepdims=True)
    return xc * jax.lax.rsqrt(var + LN_EPS) * g + b


def _dot(a, b):
    return jnp.dot(a, b, preferred_element_type=jnp.float32)


def _pack_rows(w):
    bits = jax.lax.bitcast_convert_type(w.astype(jnp.bfloat16), jnp.uint16).astype(jnp.uint32)
    return bits[..., 0::2, :] | (bits[..., 1::2, :] << 16)


def _wdot(a, w_packed):
    return _dot(a, pltpu.bitcast(w_packed, jnp.bfloat16))


def _layer_kernel(cfg, x_ref, xres_ref, p_ref, hist_ref, cnt_ref, w_in_ref, lnv_g_ref, lnv_b_ref, ws_ref, bs_ref,
                  pool_w_ref, pool_scale_ref, w_out_ref, b_out_ref, ln_g_ref, ln_b_ref, w_ple_ref,
                  w_pg_ref, b_pg_ref, *out_and_scratch):
    if cfg.emit_v:
        y_ref, hist_out_ref, v_out_ref, z_ref, mix_ref, cat_ref = out_and_scratch
    else:
        y_ref, hist_out_ref, z_ref, mix_ref, cat_ref = out_and_scratch
        v_out_ref = None
    L = cfg.sub_len
    last_sub = cfg.n_sub - 1
    step = pl.program_id(0)
    j = jax.lax.rem(jnp.minimum(step, cfg.n_tiles - 1), cfg.n_inner)

    if cfg.sequential:
        @pl.when(j == 0)
        def _():
            z_ref[last_sub, L:, :] = hist_ref[0]

    if cfg.pipelined:
        @pl.when(step == 0)
        def _():
            cat_ref[...] = jnp.zeros_like(cat_ref)

    val = {}
    pieces = {}

    def piece(name):
        def register(fn):
            pieces[name] = fn
        return register

    def col_piece(n, base=0):
        return slice(base + n * COL_PIECE, base + (n + 1) * COL_PIECE)

    def proj(lo, hi):
        return _wdot(val["xb"], w_in_ref[:, lo:hi])

    @piece("v_proj")
    def _():
        val["xb"] = x_ref[0].astype(jnp.bfloat16)
        val["v"] = proj(D_A, 2 * D_A)

    @piece("u_proj")
    def _():
        val["u"] = proj(0, D_A)

    for n in range(N_COL_PIECES):
        @piece(f"ga_proj{n}")
        def _(n=n):
            val[f"ga{n}"] = proj(2 * D_A + n * COL_PIECE, 2 * D_A + (n + 1) * COL_PIECE)

    @piece("hp_proj")
    def _():
        hp = val["hp"] = proj(3 * D_A, 3 * D_A + D_B)
        if cfg.sequential:
            z_ref[0, :HIST_ROWS, :] = z_ref[last_sub, L:, :]
        else:
            z_ref[:, :HIST_ROWS, :] = hist_ref[...]
        for s in range(cfg.n_sub):
            if cfg.sequential and s > 0:
                z_ref[s, :HIST_ROWS, :] = z_ref[s - 1, L:, :]
            z_ref[s, HIST_ROWS:, :] = hp[s * L:(s + 1) * L]
        if cfg.sequential:
            hist_out_ref[0] = z_ref[last_sub, L:, :]
        else:
            for s in range(cfg.n_sub):
                hist_out_ref[s] = z_ref[s, L:, :]

    @piece("gb_proj")
    def _():
        val["gb"] = proj(3 * D_A + D_B, D_IN)

    for h in range(N_HEADS_A):
        @piece(f"ln_v{h}")
        def _(h=h):
            sl = slice(h * HEAD_A, (h + 1) * HEAD_A)
            vn = _layernorm(val["v"][:, sl], lnv_g_ref[:, sl], lnv_b_ref[:, sl])
            if v_out_ref is not None:
                v_out_ref[0, :, sl] = vn
            vnb = vn.astype(jnp.bfloat16)
            for s in range(cfg.n_sub):
                for c0 in range(0, L, MLP_CHUNK):
                    lc = min(MLP_CHUNK, L - c0)
                    r0 = s * L + c0
                    mixed = _dot(ws_ref[h, :lc, :lc], vnb[r0:r0 + lc])
                    mix_ref[r0:r0 + lc, sl] = (mixed + bs_ref[:lc, sl]).astype(jnp.bfloat16)

    for n in range(N_COL_PIECES):
        @piece(f"a_out{n}")
        def _(n=n):
            cols = col_piece(n)
            a_out = val["u"][:, cols] * mix_ref[:, cols].astype(jnp.float32) * jax.nn.silu(val[f"ga{n}"])
            cat_ref[:, cols] = a_out.astype(jnp.bfloat16)

    for g, w in enumerate(POOL_WINDOWS):
        @piece(f"pooling{g}")
        def _(g=g, w=w):
            cols, ccols = col_piece(g), col_piece(g, D_A)
            for s in range(cfg.n_sub):
                rows = slice(s * L, (s + 1) * L)
                acc = z_ref[s, :, cols]
                span = 1
                while span < w:
                    acc = acc + pltpu.roll(acc, span, axis=0)
                    span *= 2
                win = acc[HIST_ROWS:]
                hp_g = val["hp"][rows, cols]
                mix_ref[rows, ccols] = (win * (1.0 / w) - hp_g).astype(jnp.bfloat16)
                if cfg.sequential and s == 0:
                    cnt = jnp.where(j == 0, cnt_ref[:, cols], float(w))
                    head = win[:HIST_ROWS] / cnt - hp_g[:HIST_ROWS]
                    mix_ref[:HIST_ROWS, ccols] = head.astype(jnp.bfloat16)

        @piece(f"b_out{g}")
        def _(g=g):
            cols, ccols = col_piece(g), col_piece(g, D_A)
            q = _wdot(mix_ref[:, ccols], pool_w_ref[g])
            b_outp = q * pool_scale_ref[:, cols] * jax.nn.silu(val["gb"][:, cols])
            cat_ref[:, ccols] = b_outp.astype(jnp.bfloat16)

    for n in range(N_COL_PIECES):
        @piece(f"w_out{n}")
        def _(n=n):
            cols = col_piece(n)
            y = _wdot(cat_ref[...], w_out_ref[:, cols]) + b_out_ref[:, cols]
            val[f"pre{n}"] = ALPHA * xres_ref[0, :, cols] + y

    @piece("post_norm")
    def _():
        pre = jnp.concatenate([val[f"pre{n}"] for n in range(N_COL_PIECES)], axis=-1)
        xn = val["xn"] = _layernorm(pre, ln_g_ref[...], ln_b_ref[...])
        val["xnb"] = xn.astype(jnp.bfloat16)
        val["pb"] = p_ref[0].astype(jnp.bfloat16)

    for n in range(N_COL_PIECES):
        @piece(f"gate{n}")
        def _(n=n):
            cols = col_piece(n)
            gate = jax.nn.sigmoid(_wdot(val["xnb"], w_pg_ref[:, cols]) + b_pg_ref[:, cols])
            ple = _wdot(val["pb"], w_ple_ref[:, cols])
            y_ref[0, :, cols] = val["xn"][:, cols] + gate * ple

    assert set(pieces) == set(STAGE1_PIECES + STAGE2_PIECES)
    for name in (PIPELINE_ORDER if cfg.pipelined else STAGE1_PIECES + STAGE2_PIECES):
        pieces[name]()


def _resident(shape):
    nd = len(shape)
    return pl.BlockSpec(shape, lambda i: (0,) * nd, pipeline_mode=pl.Buffered(1))


def _run_layer(cfg, x, p, hist, cnt, weights):
    T = cfg.tile_rows
    n_seq = cfg.n_tiles // cfg.n_inner
    n_hist = 1 if cfg.sequential else cfg.n_sub
    assert x.shape == (n_seq, cfg.n_inner * T, D_MODEL) and p.shape == (n_seq, cfg.n_inner * T, D_PLE)
    assert hist.shape == (n_seq * n_hist, HIST_ROWS, D_B)
    assert cfg.sequential or cfg.n_inner == 1

    def cur(i):
        t = jnp.minimum(i, cfg.n_tiles - 1)
        return t // cfg.n_inner, t % cfg.n_inner

    def res(i):
        t = jnp.maximum(i - 1, 0) if cfg.pipelined else i
        return t // cfg.n_inner, t % cfg.n_inner

    in_specs = [
        pl.BlockSpec((1, T, D_MODEL), lambda i: (*cur(i), 0)),
        pl.BlockSpec((1, T, D_MODEL), lambda i: (*res(i), 0)),
        pl.BlockSpec((1, T, D_PLE), lambda i: (*res(i), 0)),
        pl.BlockSpec((n_hist, HIST_ROWS, D_B), lambda i: (cur(i)[0], 0, 0)),
        _resident(cnt.shape),
    ] + [_resident(w.shape) for w in weights]
    out_shape = [jax.ShapeDtypeStruct(x.shape, jnp.float32),
                 jax.ShapeDtypeStruct(hist.shape, jnp.float32)]
    out_specs = [pl.BlockSpec((1, T, D_MODEL), lambda i: (*res(i), 0)),
                 pl.BlockSpec((n_hist, HIST_ROWS, D_B), lambda i: (cur(i)[0], 0, 0))]
    if cfg.emit_v:
        out_shape.append(jax.ShapeDtypeStruct(x.shape[:2] + (D_A,), jnp.float32))
        out_specs.append(pl.BlockSpec((1, T, D_A), lambda i: (*cur(i), 0)))
    return pl.pallas_call(
        functools.partial(_layer_kernel, cfg),
        grid=(cfg.n_tiles + (1 if cfg.pipelined else 0),),
        in_specs=in_specs,
        out_specs=out_specs,
        out_shape=out_shape,
        scratch_shapes=[pltpu.VMEM((cfg.n_sub, HIST_ROWS + cfg.sub_len, D_B), jnp.float32),
                        pltpu.VMEM((T, D_MIX), jnp.bfloat16),
                        pltpu.VMEM((T, D_MIX), jnp.bfloat16)],
        compiler_params=pltpu.CompilerParams(dimension_semantics=("arbitrary",),
                                             vmem_limit_bytes=VMEM_LIMIT_BYTES),
        name="layer_sample" if cfg.emit_v else "layer_prompt",
    )(x, x, p, hist, cnt, *weights)


def kernel(x_prompt, x_sample, state_pool, p_prompt, p_sample, w_in, ln_v_g, ln_v_b, w_s, b_s, pool_w, pool_scale, w_out, b_out, ln_g, ln_b, w_ple, w_pg, b_pg):
    assert DEPTH == 1 and w_in.shape[0] == DEPTH
    B, S, _ = x_prompt.shape
    Bs, Ss, _ = x_sample.shape
    bf = jnp.bfloat16
    row = lambda a: a.reshape(1, -1).astype(jnp.float32)

    blk = np.arange(MLP_CHUNK) // CHUNK
    mask = jnp.asarray(blk[None, :] <= blk[:, None])
    ws_masked = jnp.where(mask[None], w_s[0], jnp.zeros((), w_s.dtype)).astype(bf)
    bs_tbl = jnp.repeat(jnp.transpose(b_s[0]), HEAD_A, axis=1).astype(jnp.float32)
    win = np.repeat(np.asarray(POOL_WINDOWS, np.float32), POOL_GROUP)[None, :]
    cnt = jnp.asarray(np.minimum(np.arange(1, HIST_ROWS + 1, dtype=np.float32)[:, None], win))
    weights = (_pack_rows(w_in[0]), row(ln_v_g[0]), row(ln_v_b[0]), ws_masked, bs_tbl, _pack_rows(pool_w[0]),
               row(pool_scale[0]), _pack_rows(w_out[0]), row(b_out[0]), row(ln_g[0]), row(ln_b[0]),
               _pack_rows(w_ple[0]), _pack_rows(w_pg[0]), row(b_pg[0]))

    n_inner = S // PROMPT_TILE_ROWS
    cfg_p = TileConfig(n_sub=1, sub_len=PROMPT_TILE_ROWS, n_inner=n_inner, n_tiles=B * n_inner,
                       sequential=True, pipelined=True, emit_v=False)
    hist_p = jnp.zeros((B, HIST_ROWS, D_B), jnp.float32)
    y_p, hist_p_new = _run_layer(cfg_p, x_prompt, p_prompt[0], hist_p, cnt, weights)

    assert PAST_LEN >= POOL_HIST
    ns = SAMPLE_STREAMS_PER_TILE
    cfg_s = TileConfig(n_sub=ns, sub_len=Ss, n_inner=1, n_tiles=Bs // ns,
                       sequential=False, pipelined=False, emit_v=True)
    hist_s = jnp.pad(state_pool[0], ((0, 0), (1, 0), (0, 0)))
    y_s, hist_s_new, v_s = _run_layer(
        cfg_s, x_sample.reshape(Bs // ns, ns * Ss, D_MODEL),
        p_sample[0].reshape(Bs // ns, ns * Ss, D_PLE), hist_s, cnt, weights)

    return (y_p,
            y_s.reshape(Bs, Ss, D_MODEL),
            hist_p_new[None, :, 1:, :],
            hist_s_new[None, :, 1:, :],
            v_s.reshape(1, Bs, Ss, N_HEADS_A, HEAD_A))
```

```python
import functools
from typing import NamedTuple

import numpy as np
import jax
import jax.numpy as jnp
from jax.experimental import pallas as pl
from jax.experimental.pallas import tpu as pltpu

D_MODEL = 1024
DEPTH = 1
PAST_LEN = 4096
CHUNK = 64
MLP_CHUNK = 128
D_MIX = 2 * D_MODEL
D_A = D_MIX // 2
HEAD_A = 128
N_HEADS_A = D_A // HEAD_A
D_B = D_MIX - D_A
POOL_WINDOWS = (2, 4, 8, 16)
N_POOL_GROUPS = len(POOL_WINDOWS)
POOL_GROUP = D_B // N_POOL_GROUPS
POOL_HIST = max(POOL_WINDOWS) - 1
HIST_ROWS = POOL_HIST + 1
D_PLE = 256
D_IN = 3 * D_A + 2 * D_B
ALPHA = (2.0 * DEPTH) ** 0.25
LN_EPS = 1e-5

PROMPT_TILE_ROWS = 512
SAMPLE_STREAMS_PER_TILE = 8
VMEM_LIMIT_BYTES = 56 * 1024 * 1024
COL_PIECE = 256
N_COL_PIECES = D_MODEL // COL_PIECE
assert COL_PIECE == POOL_GROUP and COL_PIECE == 2 * HEAD_A
STAGE_ROWS, STAGE_COLS = 256, 1024
BIG_WEIGHT_SHAPES = ((D_MODEL, D_IN), (D_MIX, D_MODEL), (D_PLE, D_MODEL), (D_MODEL, D_MODEL))


def _numbered(name, n):
    return tuple(f"{name}{i}" for i in range(n))


STAGE1_PIECES = (("v_proj", "u_proj") + _numbered("ga_proj", N_COL_PIECES) + _numbered("ln_v", N_HEADS_A)
                 + _numbered("a_out", N_COL_PIECES) + ("hp_proj", "gb_proj") + _numbered("pooling", N_POOL_GROUPS)
                 + _numbered("b_out", N_POOL_GROUPS))
STAGE2_PIECES = _numbered("w_out", N_COL_PIECES) + ("post_norm",) + _numbered("gate", N_COL_PIECES)
PIPELINE_ORDER = (
    "v_proj", "hp_proj", "w_out0", "w_out1", "w_out2", "w_out3", "pooling0", "pooling1", "pooling2", "pooling3",
    "u_proj",
    "ln_v0", "ln_v1", "ga_proj0", "ln_v2", "ln_v3", "ga_proj1",
    "ln_v4", "ln_v5", "ga_proj2", "ln_v6", "ln_v7", "ga_proj3",
    "post_norm", "a_out0", "a_out1", "a_out2", "a_out3",
    "gate0", "gate1", "gate2", "gate3", "gb_proj",
    "b_out0", "b_out1", "b_out2", "b_out3")
assert sorted(PIPELINE_ORDER) == sorted(STAGE1_PIECES + STAGE2_PIECES)


class TileConfig(NamedTuple):
    n_sub: int
    sub_len: int
    n_inner: int
    n_tiles: int
    sequential: bool
    pipelined: bool
    emit_v: bool

    @property
    def tile_rows(self):
        return self.n_sub * self.sub_len


def _layernorm(x, g, b):
    mu = jnp.mean(x, axis=-1, keepdims=True)
    xc = x - mu
    var = jnp.mean(xc * xc, axis=-1, keepdims=True)
    return xc * jax.lax.rsqrt(var + LN_EPS) * g + b


def _dot(a, b):
    return jnp.dot(a, b, preferred_element_type=jnp.float32)


def _stage_weights(hbm_refs, vmem_refs, stage_ref, sem):
    chunks = [(src, dst, r, c)
              for src, dst in zip(hbm_refs, vmem_refs)
              for r in range(0, src.shape[0], STAGE_ROWS)
              for c in range(0, src.shape[1], STAGE_COLS)]

    def copy(k):
        src, _, r, c = chunks[k]
        return pltpu.make_async_copy(src.at[pl.ds(r, STAGE_ROWS), pl.ds(c, STAGE_COLS)],
                                     stage_ref.at[k % 2], sem.at[k % 2])

    copy(0).start()
    for k, (_, dst, r, c) in enumerate(chunks):
        if k + 1 < len(chunks):
            copy(k + 1).start()
        copy(k).wait()
        dst[r:r + STAGE_ROWS, c:c + STAGE_COLS] = stage_ref[k % 2].astype(jnp.bfloat16)


def _layer_kernel(cfg, x_ref, xres_ref, p_ref, hist_ref, cnt_ref, lnv_g_ref, lnv_b_ref, ws_ref, bs_ref,
                  pool_w_ref, pool_scale_ref, b_out_ref, ln_g_ref, ln_b_ref, b_pg_ref,
                  w_in_hbm, w_out_hbm, w_ple_hbm, w_pg_hbm, *out_and_scratch):
    *outs, z_ref, mix_ref, cat_ref, w_in_ref, w_out_ref, w_ple_ref, w_pg_ref, stage_ref, stage_sem = out_and_scratch
    if cfg.emit_v:
        y_ref, hist_out_ref, v_out_ref = outs
    else:
        y_ref, hist_out_ref = outs
        v_out_ref = None
    L = cfg.sub_len
    last_sub = cfg.n_sub - 1
    step = pl.program_id(0)
    j = jax.lax.rem(jnp.minimum(step, cfg.n_tiles - 1), cfg.n_inner)

    @pl.when(step == 0)
    def _():
        _stage_weights((w_in_hbm, w_out_hbm, w_ple_hbm, w_pg_hbm), (w_in_ref, w_out_ref, w_ple_ref, w_pg_ref),
                       stage_ref, stage_sem)
        if cfg.pipelined:
            cat_ref[...] = jnp.zeros_like(cat_ref)

    if cfg.sequential:
        @pl.when(j == 0)
        def _():
            z_ref[last_sub, L:, :] = hist_ref[0]

    val = {}
    pieces = {}

    def piece(name):
        def register(fn):
            pieces[name] = fn
        return register

    def col_piece(n, base=0):
        return slice(base + n * COL_PIECE, base + (n + 1) * COL_PIECE)

    def proj(lo, hi):
        return _dot(val["xb"], w_in_ref[:, lo:hi])

    @piece("v_proj")
    def _():
        val["xb"] = x_ref[0].astype(jnp.bfloat16)
        val["v"] = proj(D_A, 2 * D_A)

    @piece("u_proj")
    def _():
        val["u"] = proj(0, D_A)

    for n in range(N_COL_PIECES):
        @piece(f"ga_proj{n}")
        def _(n=n):
            val[f"ga{n}"] = proj(2 * D_A + n * COL_PIECE, 2 * D_A + (n + 1) * COL_PIECE)

    @piece("hp_proj")
    def _():
        hp = val["hp"] = proj(3 * D_A, 3 * D_A + D_B)
        if cfg.sequential:
            z_ref[0, :HIST_ROWS, :] = z_ref[last_sub, L:, :]
        else:
            z_ref[:, :HIST_ROWS, :] = hist_ref[...]
        for s in range(cfg.n_sub):
            if cfg.sequential and s > 0:
                z_ref[s, :HIST_ROWS, :] = z_ref[s - 1, L:, :]
            z_ref[s, HIST_ROWS:, :] = hp[s * L:(s + 1) * L]
        if cfg.sequential:
            hist_out_ref[0] = z_ref[last_sub, L:, :]
        else:
            for s in range(cfg.n_sub):
                hist_out_ref[s] = z_ref[s, L:, :]

    @piece("gb_proj")
    def _():
        val["gb"] = proj(3 * D_A + D_B, D_IN)

    for h in range(N_HEADS_A):
        @piece(f"ln_v{h}")
        def _(h=h):
            sl = slice(h * HEAD_A, (h + 1) * HEAD_A)
            vn = _layernorm(val["v"][:, sl], lnv_g_ref[:, sl], lnv_b_ref[:, sl])
            if v_out_ref is not None:
                v_out_ref[0, :, sl] = vn
            vnb = vn.astype(jnp.bfloat16)
            for s in range(cfg.n_sub):
                for c0 in range(0, L, MLP_CHUNK):
                    lc = min(MLP_CHUNK, L - c0)
                    r0 = s * L + c0
                    mixed = _dot(ws_ref[h, :lc, :lc], vnb[r0:r0 + lc])
                    mix_ref[r0:r0 + lc, sl] = (mixed + bs_ref[:lc, sl]).astype(jnp.bfloat16)

    for n in range(N_COL_PIECES):
        @piece(f"a_out{n}")
        def _(n=n):
            cols = col_piece(n)
            a_out = val["u"][:, cols] * mix_ref[:, cols].astype(jnp.float32) * jax.nn.silu(val[f"ga{n}"])
            cat_ref[:, cols] = a_out.astype(jnp.bfloat16)

    for g, w in enumerate(POOL_WINDOWS):
        @piece(f"pooling{g}")
        def _(g=g, w=w):
            cols, ccols = col_piece(g), col_piece(g, D_A)
            for s in range(cfg.n_sub):
                rows = slice(s * L, (s + 1) * L)
                acc = z_ref[s, :, cols]
                span = 1
                while span < w:
                    acc = acc + pltpu.roll(acc, span, axis=0)
                    span *= 2
                win = acc[HIST_ROWS:]
                hp_g = val["hp"][rows, cols]
                mix_ref[rows, ccols] = (win * (1.0 / w) - hp_g).astype(jnp.bfloat16)
                if cfg.sequential and s == 0:
                    cnt = jnp.where(j == 0, cnt_ref[:, cols], float(w))
                    head = win[:HIST_ROWS] / cnt - hp_g[:HIST_ROWS]
                    mix_ref[:HIST_ROWS, ccols] = head.astype(jnp.bfloat16)

        @piece(f"b_out{g}")
        def _(g=g):
            cols, ccols = col_piece(g), col_piece(g, D_A)
            q = _dot(mix_ref[:, ccols], pool_w_ref[g])
            b_outp = q * pool_scale_ref[:, cols] * jax.nn.silu(val["gb"][:, cols])
            cat_ref[:, ccols] = b_outp.astype(jnp.bfloat16)

    for n in range(N_COL_PIECES):
        @piece(f"w_out{n}")
        def _(n=n):
            cols = col_piece(n)
            y = _dot(cat_ref[...], w_out_ref[:, cols]) + b_out_ref[:, cols]
            val[f"pre{n}"] = ALPHA * xres_ref[0, :, cols] + y

    @piece("post_norm")
    def _():
        pre = jnp.concatenate([val[f"pre{n}"] for n in range(N_COL_PIECES)], axis=-1)
        xn = val["xn"] = _layernorm(pre, ln_g_ref[...], ln_b_ref[...])
        val["xnb"] = xn.astype(jnp.bfloat16)
        val["pb"] = p_ref[0].astype(jnp.bfloat16)

    for n in range(N_COL_PIECES):
        @piece(f"gate{n}")
        def _(n=n):
            cols = col_piece(n)
            gate = jax.nn.sigmoid(_dot(val["xnb"], w_pg_ref[:, cols]) + b_pg_ref[:, cols])
            ple = _dot(val["pb"], w_ple_ref[:, cols])
            y_ref[0, :, cols] = val["xn"][:, cols] + gate * ple

    assert set(pieces) == set(STAGE1_PIECES + STAGE2_PIECES)
    for name in (PIPELINE_ORDER if cfg.pipelined else STAGE1_PIECES + STAGE2_PIECES):
        pieces[name]()


def _resident(shape):
    nd = len(shape)
    return pl.BlockSpec(shape, lambda i: (0,) * nd, pipeline_mode=pl.Buffered(1))


def _run_layer(cfg, x, p, hist, small, big):
    T = cfg.tile_rows
    n_seq = cfg.n_tiles // cfg.n_inner
    n_hist = 1 if cfg.sequential else cfg.n_sub
    assert x.shape == (n_seq, cfg.n_inner * T, D_MODEL) and p.shape == (n_seq, cfg.n_inner * T, D_PLE)
    assert hist.shape == (n_seq * n_hist, HIST_ROWS, D_B)
    assert cfg.sequential or cfg.n_inner == 1
    assert tuple(w.shape for w in big) == BIG_WEIGHT_SHAPES and all(w.dtype == jnp.float32 for w in big)
    assert all(k % STAGE_ROWS == 0 and n % STAGE_COLS == 0 for k, n in BIG_WEIGHT_SHAPES)

    def cur(i):
        t = jnp.minimum(i, cfg.n_tiles - 1)
        return t // cfg.n_inner, t % cfg.n_inner

    def res(i):
        t = jnp.maximum(i - 1, 0) if cfg.pipelined else i
        return t // cfg.n_inner, t % cfg.n_inner

    in_specs = [
        pl.BlockSpec((1, T, D_MODEL), lambda i: (*cur(i), 0)),
        pl.BlockSpec((1, T, D_MODEL), lambda i: (*res(i), 0)),
        pl.BlockSpec((1, T, D_PLE), lambda i: (*res(i), 0)),
        pl.BlockSpec((n_hist, HIST_ROWS, D_B), lambda i: (cur(i)[0], 0, 0)),
    ] + [_resident(a.shape) for a in small] + [pl.BlockSpec(memory_space=pl.ANY) for _ in big]
    out_shape = [jax.ShapeDtypeStruct(x.shape, jnp.float32),
                 jax.ShapeDtypeStruct(hist.shape, jnp.float32)]
    out_specs = [pl.BlockSpec((1, T, D_MODEL), lambda i: (*res(i), 0)),
                 pl.BlockSpec((n_hist, HIST_ROWS, D_B), lambda i: (cur(i)[0], 0, 0))]
    if cfg.emit_v:
        out_shape.append(jax.ShapeDtypeStruct(x.shape[:2] + (D_A,), jnp.float32))
        out_specs.append(pl.BlockSpec((1, T, D_A), lambda i: (*cur(i), 0)))
    return pl.pallas_call(
        functools.partial(_layer_kernel, cfg),
        grid=(cfg.n_tiles + (1 if cfg.pipelined else 0),),
        in_specs=in_specs,
        out_specs=out_specs,
        out_shape=out_shape,
        scratch_shapes=[pltpu.VMEM((cfg.n_sub, HIST_ROWS + cfg.sub_len, D_B), jnp.float32),
                        pltpu.VMEM((T, D_MIX), jnp.bfloat16),
                        pltpu.VMEM((T, D_MIX), jnp.bfloat16)]
                       + [pltpu.VMEM(s, jnp.bfloat16) for s in BIG_WEIGHT_SHAPES]
                       + [pltpu.VMEM((2, STAGE_ROWS, STAGE_COLS), jnp.float32),
                          pltpu.SemaphoreType.DMA((2,))],
        compiler_params=pltpu.CompilerParams(dimension_semantics=("arbitrary",),
                                             vmem_limit_bytes=VMEM_LIMIT_BYTES),
        name="layer_sample" if cfg.emit_v else "layer_prompt",
    )(x, x, p, hist, *small, *big)


def kernel(x_prompt, x_sample, state_pool, p_prompt, p_sample, w_in, ln_v_g, ln_v_b, w_s, b_s, pool_w, pool_scale, w_out, b_out, ln_g, ln_b, w_ple, w_pg, b_pg):
    assert DEPTH == 1 and w_in.shape[0] == DEPTH
    B, S, _ = x_prompt.shape
    Bs, Ss, _ = x_sample.shape
    bf = jnp.bfloat16
    row = lambda a: a.reshape(1, -1).astype(jnp.float32)

    blk = np.arange(MLP_CHUNK) // CHUNK
    mask = jnp.asarray(blk[None, :] <= blk[:, None])
    ws_masked = jnp.where(mask[None], w_s[0], jnp.zeros((), w_s.dtype)).astype(bf)
    bs_tbl = jnp.repeat(jnp.transpose(b_s[0]), HEAD_A, axis=1).astype(jnp.float32)
    win = np.repeat(np.asarray(POOL_WINDOWS, np.float32), POOL_GROUP)[None, :]
    cnt = jnp.asarray(np.minimum(np.arange(1, HIST_ROWS + 1, dtype=np.float32)[:, None], win))
    small = (cnt, row(ln_v_g[0]), row(ln_v_b[0]), ws_masked, bs_tbl, pool_w[0].astype(bf), row(pool_scale[0]),
             row(b_out[0]), row(ln_g[0]), row(ln_b[0]), row(b_pg[0]))
    big = (w_in[0], w_out[0], w_ple[0], w_pg[0])

    n_inner = S // PROMPT_TILE_ROWS
    cfg_p = TileConfig(n_sub=1, sub_len=PROMPT_TILE_ROWS, n_inner=n_inner, n_tiles=B * n_inner,
                       sequential=True, pipelined=True, emit_v=False)
    hist_p = jnp.zeros((B, HIST_ROWS, D_B), jnp.float32)
    y_p, hist_p_new = _run_layer(cfg_p, x_prompt, p_prompt[0], hist_p, small, big)

    assert PAST_LEN >= POOL_HIST
    ns = SAMPLE_STREAMS_PER_TILE
    cfg_s = TileConfig(n_sub=ns, sub_len=Ss, n_inner=1, n_tiles=Bs // ns,
                       sequential=False, pipelined=False, emit_v=True)
    hist_s = jnp.pad(state_pool[0], ((0, 0), (1, 0), (0, 0)))
    y_s, hist_s_new, v_s = _run_layer(
        cfg_s, x_sample.reshape(Bs // ns, ns * Ss, D_MODEL),
        p_sample[0].reshape(Bs // ns, ns * Ss, D_PLE), hist_s, small, big)

    return (y_p,
            y_s.reshape(Bs, Ss, D_MODEL),
            hist_p_new[None, :, 1:, :],
            hist_s_new[None, :, 1:, :],
            v_s.reshape(1, Bs, Ss, N_HEADS_A, HEAD_A))
```

```python
import functools
from typing import NamedTuple

import numpy as np
import jax
import jax.numpy as jnp
from jax.experimental import pallas as pl
from jax.experimental.pallas import tpu as pltpu

D_MODEL = 1024
DEPTH = 1
PAST_LEN = 4096
CHUNK = 64
MLP_CHUNK = 128
D_MIX = 2 * D_MODEL
D_A = D_MIX // 2
HEAD_A = 128
N_HEADS_A = D_A // HEAD_A
D_B = D_MIX - D_A
POOL_WINDOWS = (2, 4, 8, 16)
N_POOL_GROUPS = len(POOL_WINDOWS)
POOL_GROUP = D_B // N_POOL_GROUPS
POOL_HIST = max(POOL_WINDOWS) - 1
HIST_ROWS = POOL_HIST + 1
D_PLE = 256
D_IN = 3 * D_A + 2 * D_B
ALPHA = (2.0 * DEPTH) ** 0.25
LN_EPS = 1e-5

PROMPT_TILE_ROWS = 512
SAMPLE_STREAMS_PER_TILE = 8
VMEM_LIMIT_BYTES = 56 * 1024 * 1024
COL_PIECE = 256
N_COL_PIECES = D_MODEL // COL_PIECE
assert COL_PIECE == POOL_GROUP and COL_PIECE == 2 * HEAD_A
STAGE_ROWS, STAGE_COLS = 256, 1024
STAGE_SLOTS = 8
BIG_WEIGHT_SHAPES = ((D_MODEL, D_IN), (D_MIX, D_MODEL), (D_PLE, D_MODEL), (D_MODEL, D_MODEL))


def _numbered(name, n):
    return tuple(f"{name}{i}" for i in range(n))


STAGE1_PIECES = (("v_proj", "u_proj") + _numbered("ga_proj", N_COL_PIECES) + _numbered("ln_v", N_HEADS_A)
                 + _numbered("a_out", N_COL_PIECES) + ("hp_proj", "gb_proj") + _numbered("pooling", N_POOL_GROUPS)
                 + _numbered("b_out", N_POOL_GROUPS))
STAGE2_PIECES = _numbered("w_out", N_COL_PIECES) + ("post_norm",) + _numbered("gate", N_COL_PIECES)
PIPELINE_ORDER = (
    "v_proj", "hp_proj", "w_out0", "w_out1", "w_out2", "w_out3", "pooling0", "pooling1", "pooling2", "pooling3",
    "u_proj",
    "ln_v0", "ln_v1", "ga_proj0", "ln_v2", "ln_v3", "ga_proj1",
    "ln_v4", "ln_v5", "ga_proj2", "ln_v6", "ln_v7", "ga_proj3",
    "post_norm", "a_out0", "a_out1", "a_out2", "a_out3",
    "gate0", "gate1", "gate2", "gate3", "gb_proj",
    "b_out0", "b_out1", "b_out2", "b_out3")
assert sorted(PIPELINE_ORDER) == sorted(STAGE1_PIECES + STAGE2_PIECES)


class TileConfig(NamedTuple):
    n_sub: int
    sub_len: int
    n_inner: int
    n_tiles: int
    sequential: bool
    pipelined: bool
    emit_v: bool

    @property
    def tile_rows(self):
        return self.n_sub * self.sub_len


def _layernorm(x, g, b):
    mu = jnp.mean(x, axis=-1, keepdims=True)
    xc = x - mu
    var = jnp.mean(xc * xc, axis=-1, keepdims=True)
    return xc * jax.lax.rsqrt(var + LN_EPS) * g + b


def _dot(a, b):
    return jnp.dot(a, b, preferred_element_type=jnp.float32)


def _stage_weights(hbm_refs, vmem_refs, stage_ref, sem):
    n_slots = stage_ref.shape[0]
    chunks = [(src, dst, r, c)
              for src, dst in zip(hbm_refs, vmem_refs)
              for r in range(0, src.shape[0], STAGE_ROWS)
              for c in range(0, src.shape[1], STAGE_COLS)]

    def copy(k):
        src, _, r, c = chunks[k]
        return pltpu.make_async_copy(src.at[pl.ds(r, STAGE_ROWS), pl.ds(c, STAGE_COLS)],
                                     stage_ref.at[k % n_slots], sem.at[k % n_slots])

    for k in range(min(n_slots - 1, len(chunks))):
        copy(k).start()
    for k, (_, dst, r, c) in enumerate(chunks):
        if k + n_slots - 1 < len(chunks):
            copy(k + n_slots - 1).start()
        copy(k).wait()
        dst[r:r + STAGE_ROWS, c:c + STAGE_COLS] = stage_ref[k % n_slots].astype(jnp.bfloat16)


def _layer_kernel(cfg, x_ref, xres_ref, p_ref, hist_ref, cnt_ref, lnv_g_ref, lnv_b_ref, ws_ref, bs_ref,
                  pool_w_ref, pool_scale_ref, b_out_ref, ln_g_ref, ln_b_ref, b_pg_ref,
                  w_in_hbm, w_out_hbm, w_ple_hbm, w_pg_hbm, *out_and_scratch):
    *outs, z_ref, mix_ref, cat_ref, w_in_ref, w_out_ref, w_ple_ref, w_pg_ref, stage_ref, stage_sem = out_and_scratch
    if cfg.emit_v:
        y_ref, hist_out_ref, v_out_ref = outs
    else:
        y_ref, hist_out_ref = outs
        v_out_ref = None
    L = cfg.sub_len
    last_sub = cfg.n_sub - 1
    step = pl.program_id(0)
    j = jax.lax.rem(jnp.minimum(step, cfg.n_tiles - 1), cfg.n_inner)

    @pl.when(step == 0)
    def _():
        _stage_weights((w_in_hbm, w_out_hbm, w_ple_hbm, w_pg_hbm), (w_in_ref, w_out_ref, w_ple_ref, w_pg_ref),
                       stage_ref, stage_sem)
        if cfg.pipelined:
            cat_ref[...] = jnp.zeros_like(cat_ref)

    if cfg.sequential:
        @pl.when(j == 0)
        def _():
            z_ref[last_sub, L:, :] = hist_ref[0]

    val = {}
    pieces = {}

    def piece(name):
        def register(fn):
            pieces[name] = fn
        return register

    def col_piece(n, base=0):
        return slice(base + n * COL_PIECE, base + (n + 1) * COL_PIECE)

    def proj(lo, hi):
        return _dot(val["xb"], w_in_ref[:, lo:hi])

    @piece("v_proj")
    def _():
        val["xb"] = x_ref[0].astype(jnp.bfloat16)
        val["v"] = proj(D_A, 2 * D_A)

    @piece("u_proj")
    def _():
        val["u"] = proj(0, D_A)

    for n in range(N_COL_PIECES):
        @piece(f"ga_proj{n}")
        def _(n=n):
            val[f"ga{n}"] = proj(2 * D_A + n * COL_PIECE, 2 * D_A + (n + 1) * COL_PIECE)

    @piece("hp_proj")
    def _():
        hp = val["hp"] = proj(3 * D_A, 3 * D_A + D_B)
        if cfg.sequential:
            z_ref[0, :HIST_ROWS, :] = z_ref[last_sub, L:, :]
        else:
            z_ref[:, :HIST_ROWS, :] = hist_ref[...]
        for s in range(cfg.n_sub):
            if cfg.sequential and s > 0:
                z_ref[s, :HIST_ROWS, :] = z_ref[s - 1, L:, :]
            z_ref[s, HIST_ROWS:, :] = hp[s * L:(s + 1) * L]
        if cfg.sequential:
            hist_out_ref[0] = z_ref[last_sub, L:, :]
        else:
            for s in range(cfg.n_sub):
                hist_out_ref[s] = z_ref[s, L:, :]

    @piece("gb_proj")
    def _():
        val["gb"] = proj(3 * D_A + D_B, D_IN)

    for h in range(N_HEADS_A):
        @piece(f"ln_v{h}")
        def _(h=h):
            sl = slice(h * HEAD_A, (h + 1) * HEAD_A)
            vn = _layernorm(val["v"][:, sl], lnv_g_ref[:, sl], lnv_b_ref[:, sl])
            if v_out_ref is not None:
                v_out_ref[0, :, sl] = vn
            vnb = vn.astype(jnp.bfloat16)
            for s in range(cfg.n_sub):
                for c0 in range(0, L, MLP_CHUNK):
                    lc = min(MLP_CHUNK, L - c0)
                    r0 = s * L + c0
                    mixed = _dot(ws_ref[h, :lc, :lc], vnb[r0:r0 + lc])
                    mix_ref[r0:r0 + lc, sl] = (mixed + bs_ref[:lc, sl]).astype(jnp.bfloat16)

    for n in range(N_COL_PIECES):
        @piece(f"a_out{n}")
        def _(n=n):
            cols = col_piece(n)
            a_out = val["u"][:, cols] * mix_ref[:, cols].astype(jnp.float32) * jax.nn.silu(val[f"ga{n}"])
            cat_ref[:, cols] = a_out.astype(jnp.bfloat16)

    for g, w in enumerate(POOL_WINDOWS):
        @piece(f"pooling{g}")
        def _(g=g, w=w):
            cols, ccols = col_piece(g), col_piece(g, D_A)
            for s in range(cfg.n_sub):
                rows = slice(s * L, (s + 1) * L)
                acc = z_ref[s, :, cols]
                span = 1
                while span < w:
                    acc = acc + pltpu.roll(acc, span, axis=0)
                    span *= 2
                win = acc[HIST_ROWS:]
                hp_g = val["hp"][rows, cols]
                mix_ref[rows, ccols] = (win * (1.0 / w) - hp_g).astype(jnp.bfloat16)
                if cfg.sequential and s == 0:
                    cnt = jnp.where(j == 0, cnt_ref[:, cols], float(w))
                    head = win[:HIST_ROWS] / cnt - hp_g[:HIST_ROWS]
                    mix_ref[:HIST_ROWS, ccols] = head.astype(jnp.bfloat16)

        @piece(f"b_out{g}")
        def _(g=g):
            cols, ccols = col_piece(g), col_piece(g, D_A)
            q = _dot(mix_ref[:, ccols], pool_w_ref[g])
            b_outp = q * pool_scale_ref[:, cols] * jax.nn.silu(val["gb"][:, cols])
            cat_ref[:, ccols] = b_outp.astype(jnp.bfloat16)

    for n in range(N_COL_PIECES):
        @piece(f"w_out{n}")
        def _(n=n):
            cols = col_piece(n)
            y = _dot(cat_ref[...], w_out_ref[:, cols]) + b_out_ref[:, cols]
            val[f"pre{n}"] = ALPHA * xres_ref[0, :, cols] + y

    @piece("post_norm")
    def _():
        pre = jnp.concatenate([val[f"pre{n}"] for n in range(N_COL_PIECES)], axis=-1)
        xn = val["xn"] = _layernorm(pre, ln_g_ref[...], ln_b_ref[...])
        val["xnb"] = xn.astype(jnp.bfloat16)
        val["pb"] = p_ref[0].astype(jnp.bfloat16)

    for n in range(N_COL_PIECES):
        @piece(f"gate{n}")
        def _(n=n):
            cols = col_piece(n)
            gate = jax.nn.sigmoid(_dot(val["xnb"], w_pg_ref[:, cols]) + b_pg_ref[:, cols])
            ple = _dot(val["pb"], w_ple_ref[:, cols])
            y_ref[0, :, cols] = val["xn"][:, cols] + gate * ple

    assert set(pieces) == set(STAGE1_PIECES + STAGE2_PIECES)
    for name in (PIPELINE_ORDER if cfg.pipelined else STAGE1_PIECES + STAGE2_PIECES):
        pieces[name]()


def _resident(shape):
    nd = len(shape)
    return pl.BlockSpec(shape, lambda i: (0,) * nd, pipeline_mode=pl.Buffered(1))


def _run_layer(cfg, x, p, hist, small, big):
    T = cfg.tile_rows
    n_seq = cfg.n_tiles // cfg.n_inner
    n_hist = 1 if cfg.sequential else cfg.n_sub
    assert x.shape == (n_seq, cfg.n_inner * T, D_MODEL) and p.shape == (n_seq, cfg.n_inner * T, D_PLE)
    assert hist.shape == (n_seq * n_hist, HIST_ROWS, D_B)
    assert cfg.sequential or cfg.n_inner == 1
    assert tuple(w.shape for w in big) == BIG_WEIGHT_SHAPES and all(w.dtype == jnp.float32 for w in big)
    assert all(k % STAGE_ROWS == 0 and n % STAGE_COLS == 0 for k, n in BIG_WEIGHT_SHAPES)

    def cur(i):
        t = jnp.minimum(i, cfg.n_tiles - 1)
        return t // cfg.n_inner, t % cfg.n_inner

    def res(i):
        t = jnp.maximum(i - 1, 0) if cfg.pipelined else i
        return t // cfg.n_inner, t % cfg.n_inner

    in_specs = [
        pl.BlockSpec((1, T, D_MODEL), lambda i: (*cur(i), 0)),
        pl.BlockSpec((1, T, D_MODEL), lambda i: (*res(i), 0)),
        pl.BlockSpec((1, T, D_PLE), lambda i: (*res(i), 0)),
        pl.BlockSpec((n_hist, HIST_ROWS, D_B), lambda i: (cur(i)[0], 0, 0)),
    ] + [_resident(a.shape) for a in small] + [pl.BlockSpec(memory_space=pl.ANY) for _ in big]
    out_shape = [jax.ShapeDtypeStruct(x.shape, jnp.float32),
                 jax.ShapeDtypeStruct(hist.shape, jnp.float32)]
    out_specs = [pl.BlockSpec((1, T, D_MODEL), lambda i: (*res(i), 0)),
                 pl.BlockSpec((n_hist, HIST_ROWS, D_B), lambda i: (cur(i)[0], 0, 0))]
    if cfg.emit_v:
        out_shape.append(jax.ShapeDtypeStruct(x.shape[:2] + (D_A,), jnp.float32))
        out_specs.append(pl.BlockSpec((1, T, D_A), lambda i: (*cur(i), 0)))
    return pl.pallas_call(
        functools.partial(_layer_kernel, cfg),
        grid=(cfg.n_tiles + (1 if cfg.pipelined else 0),),
        in_specs=in_specs,
        out_specs=out_specs,
        out_shape=out_shape,
        scratch_shapes=[pltpu.VMEM((cfg.n_sub, HIST_ROWS + cfg.sub_len, D_B), jnp.float32),
                        pltpu.VMEM((T, D_MIX), jnp.bfloat16),
                        pltpu.VMEM((T, D_MIX), jnp.bfloat16)]
                       + [pltpu.VMEM(s, jnp.bfloat16) for s in BIG_WEIGHT_SHAPES]
                       + [pltpu.VMEM((STAGE_SLOTS, STAGE_ROWS, STAGE_COLS), jnp.float32),
                          pltpu.SemaphoreType.DMA((STAGE_SLOTS,))],
        compiler_params=pltpu.CompilerParams(dimension_semantics=("arbitrary",),
                                             vmem_limit_bytes=VMEM_LIMIT_BYTES),
        name="layer_sample" if cfg.emit_v else "layer_prompt",
    )(x, x, p, hist, *small, *big)


def kernel(x_prompt, x_sample, state_pool, p_prompt, p_sample, w_in, ln_v_g, ln_v_b, w_s, b_s, pool_w, pool_scale, w_out, b_out, ln_g, ln_b, w_ple, w_pg, b_pg):
    assert DEPTH == 1 and w_in.shape[0] == DEPTH
    B, S, _ = x_prompt.shape
    Bs, Ss, _ = x_sample.shape
    bf = jnp.bfloat16
    row = lambda a: a.reshape(1, -1).astype(jnp.float32)

    blk = np.arange(MLP_CHUNK) // CHUNK
    mask = jnp.asarray(blk[None, :] <= blk[:, None])
    ws_masked = jnp.where(mask[None], w_s[0], jnp.zeros((), w_s.dtype)).astype(bf)
    bs_tbl = jnp.repeat(jnp.transpose(b_s[0]), HEAD_A, axis=1).astype(jnp.float32)
    win = np.repeat(np.asarray(POOL_WINDOWS, np.float32), POOL_GROUP)[None, :]
    cnt = jnp.asarray(np.minimum(np.arange(1, HIST_ROWS + 1, dtype=np.float32)[:, None], win))
    small = (cnt, row(ln_v_g[0]), row(ln_v_b[0]), ws_masked, bs_tbl, pool_w[0].astype(bf), row(pool_scale[0]),
             row(b_out[0]), row(ln_g[0]), row(ln_b[0]), row(b_pg[0]))
    big = (w_in[0], w_out[0], w_ple[0], w_pg[0])

    n_inner = S // PROMPT_TILE_ROWS
    cfg_p = TileConfig(n_sub=1, sub_len=PROMPT_TILE_ROWS, n_inner=n_inner, n_tiles=B * n_inner,
                       sequential=True, pipelined=True, emit_v=False)
    hist_p = jnp.zeros((B, HIST_ROWS, D_B), jnp.float32)
    y_p, hist_p_new = _run_layer(cfg_p, x_prompt, p_prompt[0], hist_p, small, big)

    assert PAST_LEN >= POOL_HIST
    ns = SAMPLE_STREAMS_PER_TILE
    cfg_s = TileConfig(n_sub=ns, sub_len=Ss, n_inner=1, n_tiles=Bs // ns,
                       sequential=False, pipelined=False, emit_v=True)
    hist_s = jnp.pad(state_pool[0], ((0, 0), (1, 0), (0, 0)))
    y_s, hist_s_new, v_s = _run_layer(
        cfg_s, x_sample.reshape(Bs // ns, ns * Ss, D_MODEL),
        p_sample[0].reshape(Bs // ns, ns * Ss, D_PLE), hist_s, small, big)

    return (y_p,
            y_s.reshape(Bs, Ss, D_MODEL),
            hist_p_new[None, :, 1:, :],
            hist_s_new[None, :, 1:, :],
            v_s.reshape(1, Bs, Ss, N_HEADS_A, HEAD_A))
```

```python
import functools
from typing import NamedTuple

import numpy as np
import jax
import jax.numpy as jnp
from jax.experimental import pallas as pl
from jax.experimental.pallas import tpu as pltpu

D_MODEL = 1024
DEPTH = 1
PAST_LEN = 4096
CHUNK = 64
MLP_CHUNK = 128
D_MIX = 2 * D_MODEL
D_A = D_MIX // 2
HEAD_A = 128
N_HEADS_A = D_A // HEAD_A
D_B = D_MIX - D_A
POOL_WINDOWS = (2, 4, 8, 16)
N_POOL_GROUPS = len(POOL_WINDOWS)
POOL_GROUP = D_B // N_POOL_GROUPS
POOL_HIST = max(POOL_WINDOWS) - 1
HIST_ROWS = POOL_HIST + 1
D_PLE = 256
D_IN = 3 * D_A + 2 * D_B
ALPHA = (2.0 * DEPTH) ** 0.25
LN_EPS = 1e-5

PROMPT_TILE_ROWS = 512
SAMPLE_STREAMS_PER_TILE = 8
VMEM_LIMIT_BYTES = 56 * 1024 * 1024
COL_PIECE = 256
N_COL_PIECES = D_MODEL // COL_PIECE
assert COL_PIECE == POOL_GROUP and COL_PIECE == 2 * HEAD_A
STAGE_ROWS, STAGE_COLS = 256, 1024
STAGE_SLOTS = 8
BIG_WEIGHT_SHAPES = ((D_MODEL, D_IN), (D_MIX, D_MODEL), (D_PLE, D_MODEL), (D_MODEL, D_MODEL))


def _numbered(name, n):
    return tuple(f"{name}{i}" for i in range(n))


STAGE1_PIECES = (("v_proj",) + _numbered("u_proj", N_COL_PIECES) + _numbered("ga_proj", N_COL_PIECES)
                 + _numbered("ln_v", N_HEADS_A) + _numbered("a_out", N_COL_PIECES) + ("hp_proj",)
                 + _numbered("gb_proj", N_POOL_GROUPS) + _numbered("pooling", N_POOL_GROUPS)
                 + _numbered("b_out", N_POOL_GROUPS))
STAGE2_PIECES = _numbered("w_out", N_COL_PIECES) + ("post_norm",) + _numbered("gate", N_COL_PIECES)
PIPELINE_ORDER = (
    "v_proj", "hp_proj", "w_out0", "w_out1", "w_out2", "w_out3", "pooling0", "pooling1", "pooling2", "pooling3",
    "ln_v0", "ln_v1", "u_proj0", "ga_proj0", "post_norm", "a_out0",
    "ln_v2", "ln_v3", "u_proj1", "ga_proj1", "a_out1",
    "ln_v4", "ln_v5", "u_proj2", "ga_proj2", "a_out2",
    "ln_v6", "ln_v7", "u_proj3", "ga_proj3", "a_out3",
    "gate0", "gate1", "gate2", "gate3",
    "gb_proj0", "b_out0", "gb_proj1", "b_out1", "gb_proj2", "b_out2", "gb_proj3", "b_out3")
assert sorted(PIPELINE_ORDER) == sorted(STAGE1_PIECES + STAGE2_PIECES)


class TileConfig(NamedTuple):
    n_sub: int
    sub_len: int
    n_inner: int
    n_tiles: int
    sequential: bool
    pipelined: bool
    emit_v: bool

    @property
    def tile_rows(self):
        return self.n_sub * self.sub_len


def _layernorm(x, g, b):
    mu = jnp.mean(x, axis=-1, keepdims=True)
    xc = x - mu
    var = jnp.mean(xc * xc, axis=-1, keepdims=True)
    return xc * jax.lax.rsqrt(var + LN_EPS) * g + b


def _dot(a, b):
    return jnp.dot(a, b, preferred_element_type=jnp.float32)


def _stage_weights(hbm_refs, vmem_refs, stage_ref, sem):
    n_slots = stage_ref.shape[0]
    chunks = [(src, dst, r, c)
              for src, dst in zip(hbm_refs, vmem_refs)
              for r in range(0, src.shape[0], STAGE_ROWS)
              for c in range(0, src.shape[1], STAGE_COLS)]

    def copy(k):
        src, _, r, c = chunks[k]
        return pltpu.make_async_copy(src.at[pl.ds(r, STAGE_ROWS), pl.ds(c, STAGE_COLS)],
                                     stage_ref.at[k % n_slots], sem.at[k % n_slots])

    for k in range(min(n_slots - 1, len(chunks))):
        copy(k).start()
    for k, (_, dst, r, c) in enumerate(chunks):
        if k + n_slots - 1 < len(chunks):
            copy(k + n_slots - 1).start()
        copy(k).wait()
        dst[r:r + STAGE_ROWS, c:c + STAGE_COLS] = stage_ref[k % n_slots].astype(jnp.bfloat16)


def _layer_kernel(cfg, x_ref, xres_ref, p_ref, hist_ref, cnt_ref, lnv_g_ref, lnv_b_ref, ws_ref, bs_ref,
                  pool_w_ref, pool_scale_ref, b_out_ref, ln_g_ref, ln_b_ref, b_pg_ref,
                  w_in_hbm, w_out_hbm, w_ple_hbm, w_pg_hbm, *out_and_scratch):
    *outs, z_ref, mix_ref, cat_ref, w_in_ref, w_out_ref, w_ple_ref, w_pg_ref, stage_ref, stage_sem = out_and_scratch
    if cfg.emit_v:
        y_ref, hist_out_ref, v_out_ref = outs
    else:
        y_ref, hist_out_ref = outs
        v_out_ref = None
    L = cfg.sub_len
    last_sub = cfg.n_sub - 1
    step = pl.program_id(0)
    j = jax.lax.rem(jnp.minimum(step, cfg.n_tiles - 1), cfg.n_inner)

    @pl.when(step == 0)
    def _():
        _stage_weights((w_in_hbm, w_out_hbm, w_ple_hbm, w_pg_hbm), (w_in_ref, w_out_ref, w_ple_ref, w_pg_ref),
                       stage_ref, stage_sem)
        if cfg.pipelined:
            cat_ref[...] = jnp.zeros_like(cat_ref)

    if cfg.sequential:
        @pl.when(j == 0)
        def _():
            z_ref[last_sub, L:, :] = hist_ref[0]

    val = {}
    pieces = {}

    def piece(name):
        def register(fn):
            pieces[name] = fn
        return register

    def col_piece(n, base=0):
        return slice(base + n * COL_PIECE, base + (n + 1) * COL_PIECE)

    def proj(lo, hi):
        return _dot(val["xb"], w_in_ref[:, lo:hi])

    @piece("v_proj")
    def _():
        val["xb"] = x_ref[0].astype(jnp.bfloat16)
        val["v"] = proj(D_A, 2 * D_A)

    for n in range(N_COL_PIECES):
        @piece(f"u_proj{n}")
        def _(n=n):
            val[f"u{n}"] = proj(n * COL_PIECE, (n + 1) * COL_PIECE)

        @piece(f"ga_proj{n}")
        def _(n=n):
            val[f"ga{n}"] = proj(2 * D_A + n * COL_PIECE, 2 * D_A + (n + 1) * COL_PIECE)

    @piece("hp_proj")
    def _():
        hp = val["hp"] = proj(3 * D_A, 3 * D_A + D_B)
        if cfg.sequential:
            z_ref[0, :HIST_ROWS, :] = z_ref[last_sub, L:, :]
        else:
            z_ref[:, :HIST_ROWS, :] = hist_ref[...]
        for s in range(cfg.n_sub):
            if cfg.sequential and s > 0:
                z_ref[s, :HIST_ROWS, :] = z_ref[s - 1, L:, :]
            z_ref[s, HIST_ROWS:, :] = hp[s * L:(s + 1) * L]
        if cfg.sequential:
            hist_out_ref[0] = z_ref[last_sub, L:, :]
        else:
            for s in range(cfg.n_sub):
                hist_out_ref[s] = z_ref[s, L:, :]

    for g in range(N_POOL_GROUPS):
        @piece(f"gb_proj{g}")
        def _(g=g):
            val[f"gb{g}"] = proj(3 * D_A + D_B + g * POOL_GROUP, 3 * D_A + D_B + (g + 1) * POOL_GROUP)

    for h in range(N_HEADS_A):
        @piece(f"ln_v{h}")
        def _(h=h):
            sl = slice(h * HEAD_A, (h + 1) * HEAD_A)
            vn = _layernorm(val["v"][:, sl], lnv_g_ref[:, sl], lnv_b_ref[:, sl])
            if v_out_ref is not None:
                v_out_ref[0, :, sl] = vn
            vnb = vn.astype(jnp.bfloat16)
            for s in range(cfg.n_sub):
                for c0 in range(0, L, MLP_CHUNK):
                    lc = min(MLP_CHUNK, L - c0)
                    r0 = s * L + c0
                    mixed = _dot(ws_ref[h, :lc, :lc], vnb[r0:r0 + lc])
                    mix_ref[r0:r0 + lc, sl] = (mixed + bs_ref[:lc, sl]).astype(jnp.bfloat16)

    for n in range(N_COL_PIECES):
        @piece(f"a_out{n}")
        def _(n=n):
            cols = col_piece(n)
            a_out = val[f"u{n}"] * mix_ref[:, cols].astype(jnp.float32) * jax.nn.silu(val[f"ga{n}"])
            cat_ref[:, cols] = a_out.astype(jnp.bfloat16)

    for g, w in enumerate(POOL_WINDOWS):
        @piece(f"pooling{g}")
        def _(g=g, w=w):
            cols, ccols = col_piece(g), col_piece(g, D_A)
            for s in range(cfg.n_sub):
                rows = slice(s * L, (s + 1) * L)
                acc = z_ref[s, :, cols]
                span = 1
                while span < w:
                    acc = acc + pltpu.roll(acc, span, axis=0)
                    span *= 2
                win = acc[HIST_ROWS:]
                hp_g = val["hp"][rows, cols]
                mix_ref[rows, ccols] = (win * (1.0 / w) - hp_g).astype(jnp.bfloat16)
                if cfg.sequential and s == 0:
                    cnt = jnp.where(j == 0, cnt_ref[:, cols], float(w))
                    head = win[:HIST_ROWS] / cnt - hp_g[:HIST_ROWS]
                    mix_ref[:HIST_ROWS, ccols] = head.astype(jnp.bfloat16)

        @piece(f"b_out{g}")
        def _(g=g):
            cols, ccols = col_piece(g), col_piece(g, D_A)
            q = _dot(mix_ref[:, ccols], pool_w_ref[g])
            b_outp = q * pool_scale_ref[:, cols] * jax.nn.silu(val[f"gb{g}"])
            cat_ref[:, ccols] = b_outp.astype(jnp.bfloat16)

    for n in range(N_COL_PIECES):
        @piece(f"w_out{n}")
        def _(n=n):
            cols = col_piece(n)
            y = _dot(cat_ref[...], w_out_ref[:, cols]) + b_out_ref[:, cols]
            val[f"pre{n}"] = ALPHA * xres_ref[0, :, cols] + y

    @piece("post_norm")
    def _():
        pre = jnp.concatenate([val[f"pre{n}"] for n in range(N_COL_PIECES)], axis=-1)
        xn = val["xn"] = _layernorm(pre, ln_g_ref[...], ln_b_ref[...])
        val["xnb"] = xn.astype(jnp.bfloat16)
        val["pb"] = p_ref[0].astype(jnp.bfloat16)

    for n in range(N_COL_PIECES):
        @piece(f"gate{n}")
        def _(n=n):
            cols = col_piece(n)
            gate = jax.nn.sigmoid(_dot(val["xnb"], w_pg_ref[:, cols]) + b_pg_ref[:, cols])
            ple = _dot(val["pb"], w_ple_ref[:, cols])
            y_ref[0, :, cols] = val["xn"][:, cols] + gate * ple

    assert set(pieces) == set(STAGE1_PIECES + STAGE2_PIECES)
    for name in (PIPELINE_ORDER if cfg.pipelined else STAGE1_PIECES + STAGE2_PIECES):
        pieces[name]()


def _resident(shape):
    nd = len(shape)
    return pl.BlockSpec(shape, lambda i: (0,) * nd, pipeline_mode=pl.Buffered(1))


def _run_layer(cfg, x, p, hist, small, big):
    T = cfg.tile_rows
    n_seq = cfg.n_tiles // cfg.n_inner
    n_hist = 1 if cfg.sequential else cfg.n_sub
    assert x.shape == (n_seq, cfg.n_inner * T, D_MODEL) and p.shape == (n_seq, cfg.n_inner * T, D_PLE)
    assert hist.shape == (n_seq * n_hist, HIST_ROWS, D_B)
    assert cfg.sequential or cfg.n_inner == 1
    assert tuple(w.shape for w in big) == BIG_WEIGHT_SHAPES and all(w.dtype == jnp.float32 for w in big)
    assert all(k % STAGE_ROWS == 0 and n % STAGE_COLS == 0 for k, n in BIG_WEIGHT_SHAPES)

    def cur(i):
        t = jnp.minimum(i, cfg.n_tiles - 1)
        return t // cfg.n_inner, t % cfg.n_inner

    def res(i):
        t = jnp.maximum(i - 1, 0) if cfg.pipelined else i
        return t // cfg.n_inner, t % cfg.n_inner

    in_specs = [
        pl.BlockSpec((1, T, D_MODEL), lambda i: (*cur(i), 0)),
        pl.BlockSpec((1, T, D_MODEL), lambda i: (*res(i), 0)),
        pl.BlockSpec((1, T, D_PLE), lambda i: (*res(i), 0)),
        pl.BlockSpec((n_hist, HIST_ROWS, D_B), lambda i: (cur(i)[0], 0, 0)),
    ] + [_resident(a.shape) for a in small] + [pl.BlockSpec(memory_space=pl.ANY) for _ in big]
    out_shape = [jax.ShapeDtypeStruct(x.shape, jnp.float32),
                 jax.ShapeDtypeStruct(hist.shape, jnp.float32)]
    out_specs = [pl.BlockSpec((1, T, D_MODEL), lambda i: (*res(i), 0)),
                 pl.BlockSpec((n_hist, HIST_ROWS, D_B), lambda i: (cur(i)[0], 0, 0))]
    if cfg.emit_v:
        out_shape.append(jax.ShapeDtypeStruct(x.shape[:2] + (D_A,), jnp.float32))
        out_specs.append(pl.BlockSpec((1, T, D_A), lambda i: (*cur(i), 0)))
    return pl.pallas_call(
        functools.partial(_layer_kernel, cfg),
        grid=(cfg.n_tiles + (1 if cfg.pipelined else 0),),
        in_specs=in_specs,
        out_specs=out_specs,
        out_shape=out_shape,
        scratch_shapes=[pltpu.VMEM((cfg.n_sub, HIST_ROWS + cfg.sub_len, D_B), jnp.float32),
                        pltpu.VMEM((T, D_MIX), jnp.bfloat16),
                        pltpu.VMEM((T, D_MIX), jnp.bfloat16)]
                       + [pltpu.VMEM(s, jnp.bfloat16) for s in BIG_WEIGHT_SHAPES]
                       + [pltpu.VMEM((STAGE_SLOTS, STAGE_ROWS, STAGE_COLS), jnp.float32),
                          pltpu.SemaphoreType.DMA((STAGE_SLOTS,))],
        compiler_params=pltpu.CompilerParams(dimension_semantics=("arbitrary",),
                                             vmem_limit_bytes=VMEM_LIMIT_BYTES),
        name="layer_sample" if cfg.emit_v else "layer_prompt",
    )(x, x, p, hist, *small, *big)


def kernel(x_prompt, x_sample, state_pool, p_prompt, p_sample, w_in, ln_v_g, ln_v_b, w_s, b_s, pool_w, pool_scale, w_out, b_out, ln_g, ln_b, w_ple, w_pg, b_pg):
    assert DEPTH == 1 and w_in.shape[0] == DEPTH
    B, S, _ = x_prompt.shape
    Bs, Ss, _ = x_sample.shape
    bf = jnp.bfloat16
    row = lambda a: a.reshape(1, -1).astype(jnp.float32)

    blk = np.arange(MLP_CHUNK) // CHUNK
    mask = jnp.asarray(blk[None, :] <= blk[:, None])
    ws_masked = jnp.where(mask[None], w_s[0], jnp.zeros((), w_s.dtype)).astype(bf)
    bs_tbl = jnp.repeat(jnp.transpose(b_s[0]), HEAD_A, axis=1).astype(jnp.float32)
    win = np.repeat(np.asarray(POOL_WINDOWS, np.float32), POOL_GROUP)[None, :]
    cnt = jnp.asarray(np.minimum(np.arange(1, HIST_ROWS + 1, dtype=np.float32)[:, None], win))
    small = (cnt, row(ln_v_g[0]), row(ln_v_b[0]), ws_masked, bs_tbl, pool_w[0].astype(bf), row(pool_scale[0]),
             row(b_out[0]), row(ln_g[0]), row(ln_b[0]), row(b_pg[0]))
    big = (w_in[0], w_out[0], w_ple[0], w_pg[0])

    n_inner = S // PROMPT_TILE_ROWS
    cfg_p = TileConfig(n_sub=1, sub_len=PROMPT_TILE_ROWS, n_inner=n_inner, n_tiles=B * n_inner,
                       sequential=True, pipelined=True, emit_v=False)
    hist_p = jnp.zeros((B, HIST_ROWS, D_B), jnp.float32)
    y_p, hist_p_new = _run_layer(cfg_p, x_prompt, p_prompt[0], hist_p, small, big)

    assert PAST_LEN >= POOL_HIST
    ns = SAMPLE_STREAMS_PER_TILE
    cfg_s = TileConfig(n_sub=ns, sub_len=Ss, n_inner=1, n_tiles=Bs // ns,
                       sequential=False, pipelined=False, emit_v=True)
    hist_s = jnp.pad(state_pool[0], ((0, 0), (1, 0), (0, 0)))
    y_s, hist_s_new, v_s = _run_layer(
        cfg_s, x_sample.reshape(Bs // ns, ns * Ss, D_MODEL),
        p_sample[0].reshape(Bs // ns, ns * Ss, D_PLE), hist_s, small, big)

    return (y_p,
            y_s.reshape(Bs, Ss, D_MODEL),
            hist_p_new[None, :, 1:, :],
            hist_s_new[None, :, 1:, :],
            v_s.reshape(1, Bs, Ss, N_HEADS_A, HEAD_A))
```

```python
import functools
from typing import NamedTuple

import numpy as np
import jax
import jax.numpy as jnp
from jax.experimental import pallas as pl
from jax.experimental.pallas import tpu as pltpu

D_MODEL = 1024
DEPTH = 1
PAST_LEN = 4096
CHUNK = 64
MLP_CHUNK = 128
D_MIX = 2 * D_MODEL
D_A = D_MIX // 2
HEAD_A = 128
N_HEADS_A = D_A // HEAD_A
D_B = D_MIX - D_A
POOL_WINDOWS = (2, 4, 8, 16)
N_POOL_GROUPS = len(POOL_WINDOWS)
POOL_GROUP = D_B // N_POOL_GROUPS
POOL_HIST = max(POOL_WINDOWS) - 1
HIST_ROWS = POOL_HIST + 1
D_PLE = 256
D_IN = 3 * D_A + 2 * D_B
ALPHA = (2.0 * DEPTH) ** 0.25
LN_EPS = 1e-5

PROMPT_TILE_ROWS = 512
SAMPLE_STREAMS_PER_TILE = 8
VMEM_LIMIT_BYTES = 56 * 1024 * 1024
COL_PIECE = 256
N_COL_PIECES = D_MODEL // COL_PIECE
assert COL_PIECE == POOL_GROUP and COL_PIECE == 2 * HEAD_A


def _numbered(name, n):
    return tuple(f"{name}{i}" for i in range(n))


STAGE1_PIECES = (("v_proj", "u_proj") + _numbered("ga_proj", N_COL_PIECES) + _numbered("ln_v", N_HEADS_A)
                 + _numbered("a_out", N_COL_PIECES) + ("hp_proj", "gb_proj") + _numbered("pooling", N_POOL_GROUPS)
                 + _numbered("b_out", N_POOL_GROUPS))
STAGE2_PIECES = _numbered("w_out", N_COL_PIECES) + ("post_norm",) + _numbered("gate", N_COL_PIECES)
PIPELINE_ORDER = (
    "v_proj", "hp_proj", "w_out0", "w_out1", "w_out2", "w_out3", "pooling0", "pooling1", "pooling2", "pooling3",
    "u_proj",
    "ln_v0", "ln_v1", "ga_proj0", "ln_v2", "ln_v3", "ga_proj1",
    "ln_v4", "ln_v5", "ga_proj2", "ln_v6", "ln_v7", "ga_proj3",
    "post_norm", "a_out0", "a_out1", "a_out2", "a_out3",
    "gate0", "gate1", "gate2", "gate3", "gb_proj",
    "b_out0", "b_out1", "b_out2", "b_out3")
assert sorted(PIPELINE_ORDER) == sorted(STAGE1_PIECES + STAGE2_PIECES)


class TileConfig(NamedTuple):
    n_sub: int
    sub_len: int
    n_inner: int
    n_tiles: int
    sequential: bool
    pipelined: bool
    emit_v: bool

    @property
    def tile_rows(self):
        return self.n_sub * self.sub_len


def _layernorm(x, g, b):
    mu = jnp.mean(x, axis=-1, keepdims=True)
    xc = x - mu
    var = jnp.mean(xc * xc, axis=-1, keepdims=True)
    return xc * jax.lax.rsqrt(var + LN_EPS) * g + b


def _dot(a, b):
    return jnp.dot(a, b, preferred_element_type=jnp.float32)


def _layer_kernel(cfg, x_ref, xres_ref, p_ref, hist_ref, cnt_ref, w_in_ref, lnv_g_ref, lnv_b_ref, ws_ref, bs_ref,
                  pool_w_ref, pool_scale_ref, w_out_ref, b_out_ref, ln_g_ref, ln_b_ref, w_ple_ref,
                  w_pg_ref, b_pg_ref, *out_and_scratch):
    if cfg.emit_v:
        y_ref, hist_out_ref, v_out_ref, z_ref, mix_ref, cat_ref = out_and_scratch
    else:
        y_ref, hist_out_ref, z_ref, mix_ref, cat_ref = out_and_scratch
        v_out_ref = None
    L = cfg.sub_len
    last_sub = cfg.n_sub - 1
    step = pl.program_id(0)
    j = jax.lax.rem(jnp.minimum(step, cfg.n_tiles - 1), cfg.n_inner)

    if cfg.sequential:
        @pl.when(j == 0)
        def _():
            z_ref[last_sub, L:, :] = hist_ref[0]

    if cfg.pipelined:
        @pl.when(step == 0)
        def _():
            cat_ref[...] = jnp.zeros_like(cat_ref)

    val = {}
    pieces = {}

    def piece(name):
        def register(fn):
            pieces[name] = fn
        return register

    def col_piece(n, base=0):
        return slice(base + n * COL_PIECE, base + (n + 1) * COL_PIECE)

    def proj(lo, hi):
        return _dot(val["xb"], w_in_ref[:, lo:hi])

    @piece("v_proj")
    def _():
        val["xb"] = x_ref[0].astype(jnp.bfloat16)
        val["v"] = proj(D_A, 2 * D_A)

    @piece("u_proj")
    def _():
        val["u"] = proj(0, D_A)

    for n in range(N_COL_PIECES):
        @piece(f"ga_proj{n}")
        def _(n=n):
            val[f"ga{n}"] = proj(2 * D_A + n * COL_PIECE, 2 * D_A + (n + 1) * COL_PIECE)

    @piece("hp_proj")
    def _():
        hp = val["hp"] = proj(3 * D_A, 3 * D_A + D_B)
        if cfg.sequential:
            z_ref[0, :HIST_ROWS, :] = z_ref[last_sub, L:, :]
        else:
            z_ref[:, :HIST_ROWS, :] = hist_ref[...]
        for s in range(cfg.n_sub):
            if cfg.sequential and s > 0:
                z_ref[s, :HIST_ROWS, :] = z_ref[s - 1, L:, :]
            z_ref[s, HIST_ROWS:, :] = hp[s * L:(s + 1) * L]
        if cfg.sequential:
            hist_out_ref[0] = z_ref[last_sub, L:, :]
        else:
            for s in range(cfg.n_sub):
                hist_out_ref[s] = z_ref[s, L:, :]

    @piece("gb_proj")
    def _():
        val["gb"] = proj(3 * D_A + D_B, D_IN)

    for h in range(N_HEADS_A):
        @piece(f"ln_v{h}")
        def _(h=h):
            sl = slice(h * HEAD_A, (h + 1) * HEAD_A)
            vn = _layernorm(val["v"][:, sl], lnv_g_ref[:, sl], lnv_b_ref[:, sl])
            if v_out_ref is not None:
                v_out_ref[0, :, sl] = vn
            vnb = vn.astype(jnp.bfloat16)
            for s in range(cfg.n_sub):
                for c0 in range(0, L, MLP_CHUNK):
                    lc = min(MLP_CHUNK, L - c0)
                    r0 = s * L + c0
                    mixed = _dot(ws_ref[h, :lc, :lc], vnb[r0:r0 + lc])
                    mix_ref[r0:r0 + lc, sl] = (mixed + bs_ref[:lc, sl]).astype(jnp.bfloat16)

    for n in range(N_COL_PIECES):
        @piece(f"a_out{n}")
        def _(n=n):
            cols = col_piece(n)
            a_out = val["u"][:, cols] * mix_ref[:, cols].astype(jnp.float32) * jax.nn.silu(val[f"ga{n}"])
            cat_ref[:, cols] = a_out.astype(jnp.bfloat16)

    for g, w in enumerate(POOL_WINDOWS):
        @piece(f"pooling{g}")
        def _(g=g, w=w):
            cols, ccols = col_piece(g), col_piece(g, D_A)
            for s in range(cfg.n_sub):
                rows = slice(s * L, (s + 1) * L)
                acc = z_ref[s, :, cols]
                span = 1
                while span < w:
                    acc = acc + pltpu.roll(acc, span, axis=0)
                    span *= 2
                win = acc[HIST_ROWS:]
                hp_g = val["hp"][rows, cols]
                mix_ref[rows, ccols] = (win * (1.0 / w) - hp_g).astype(jnp.bfloat16)
                if cfg.sequential and s == 0:
                    cnt = jnp.where(j == 0, cnt_ref[:, cols], float(w))
                    head = win[:HIST_ROWS] / cnt - hp_g[:HIST_ROWS]
                    mix_ref[:HIST_ROWS, ccols] = head.astype(jnp.bfloat16)

        @piece(f"b_out{g}")
        def _(g=g):
            cols, ccols = col_piece(g), col_piece(g, D_A)
            q = _dot(mix_ref[:, ccols], pool_w_ref[g])
            b_outp = q * pool_scale_ref[:, cols] * jax.nn.silu(val["gb"][:, cols])
            cat_ref[:, ccols] = b_outp.astype(jnp.bfloat16)

    for n in range(N_COL_PIECES):
        @piece(f"w_out{n}")
        def _(n=n):
            cols = col_piece(n)
            y = _dot(cat_ref[...], w_out_ref[:, cols]) + b_out_ref[:, cols]
            val[f"pre{n}"] = ALPHA * xres_ref[0, :, cols] + y

    @piece("post_norm")
    def _():
        pre = jnp.concatenate([val[f"pre{n}"] for n in range(N_COL_PIECES)], axis=-1)
        xn = val["xn"] = _layernorm(pre, ln_g_ref[...], ln_b_ref[...])
        val["xnb"] = xn.astype(jnp.bfloat16)
        val["pb"] = p_ref[0].astype(jnp.bfloat16)

    for n in range(N_COL_PIECES):
        @piece(f"gate{n}")
        def _(n=n):
            cols = col_piece(n)
            gate = jax.nn.sigmoid(_dot(val["xnb"], w_pg_ref[:, cols]) + b_pg_ref[:, cols])
            ple = _dot(val["pb"], w_ple_ref[:, cols])
            y_ref[0, :, cols] = val["xn"][:, cols] + gate * ple

    assert set(pieces) == set(STAGE1_PIECES + STAGE2_PIECES)
    for name in (PIPELINE_ORDER if cfg.pipelined else STAGE1_PIECES + STAGE2_PIECES):
        pieces[name]()


def _resident(shape):
    nd = len(shape)
    return pl.BlockSpec(shape, lambda i: (0,) * nd, pipeline_mode=pl.Buffered(1))


def _run_layer(cfg, x, p, hist, cnt, weights):
    T = cfg.tile_rows
    n_seq = cfg.n_tiles // cfg.n_inner
    n_hist = 1 if cfg.sequential else cfg.n_sub
    assert x.shape == (n_seq, cfg.n_inner * T, D_MODEL) and p.shape == (n_seq, cfg.n_inner * T, D_PLE)
    assert hist.shape == (n_seq * n_hist, HIST_ROWS, D_B)
    assert cfg.sequential or cfg.n_inner == 1

    def cur(i):
        t = jnp.minimum(i, cfg.n_tiles - 1)
        return t // cfg.n_inner, t % cfg.n_inner

    def res(i):
        t = jnp.maximum(i - 1, 0) if cfg.pipelined else i
        return t // cfg.n_inner, t % cfg.n_inner

    in_specs = [
        pl.BlockSpec((1, T, D_MODEL), lambda i: (*cur(i), 0)),
        pl.BlockSpec((1, T, D_MODEL), lambda i: (*res(i), 0)),
        pl.BlockSpec((1, T, D_PLE), lambda i: (*res(i), 0)),
        pl.BlockSpec((n_hist, HIST_ROWS, D_B), lambda i: (cur(i)[0], 0, 0)),
        _resident(cnt.shape),
    ] + [_resident(w.shape) for w in weights]
    out_shape = [jax.ShapeDtypeStruct(x.shape, jnp.float32),
                 jax.ShapeDtypeStruct(hist.shape, jnp.float32)]
    out_specs = [pl.BlockSpec((1, T, D_MODEL), lambda i: (*res(i), 0)),
                 pl.BlockSpec((n_hist, HIST_ROWS, D_B), lambda i: (cur(i)[0], 0, 0))]
    if cfg.emit_v:
        out_shape.append(jax.ShapeDtypeStruct(x.shape[:2] + (D_A,), jnp.float32))
        out_specs.append(pl.BlockSpec((1, T, D_A), lambda i: (*cur(i), 0)))
    return pl.pallas_call(
        functools.partial(_layer_kernel, cfg),
        grid=(cfg.n_tiles + (1 if cfg.pipelined else 0),),
        in_specs=in_specs,
        out_specs=out_specs,
        out_shape=out_shape,
        scratch_shapes=[pltpu.VMEM((cfg.n_sub, HIST_ROWS + cfg.sub_len, D_B), jnp.float32),
                        pltpu.VMEM((T, D_MIX), jnp.bfloat16),
                        pltpu.VMEM((T, D_MIX), jnp.bfloat16)],
        compiler_params=pltpu.CompilerParams(dimension_semantics=("arbitrary",),
                                             vmem_limit_bytes=VMEM_LIMIT_BYTES),
        name="layer_sample" if cfg.emit_v else "layer_prompt",
    )(x, x, p, hist, cnt, *weights)


def kernel(x_prompt, x_sample, state_pool, p_prompt, p_sample, w_in, ln_v_g, ln_v_b, w_s, b_s, pool_w, pool_scale, w_out, b_out, ln_g, ln_b, w_ple, w_pg, b_pg):
    assert DEPTH == 1 and w_in.shape[0] == DEPTH
    B, S, _ = x_prompt.shape
    Bs, Ss, _ = x_sample.shape
    bf = jnp.bfloat16
    row = lambda a: a.reshape(1, -1).astype(jnp.float32)

    blk = np.arange(MLP_CHUNK) // CHUNK
    mask = jnp.asarray(blk[None, :] <= blk[:, None])
    ws_masked = jnp.where(mask[None], w_s[0], jnp.zeros((), w_s.dtype)).astype(bf)
    bs_tbl = jnp.repeat(jnp.transpose(b_s[0]), HEAD_A, axis=1).astype(jnp.float32)
    win = np.repeat(np.asarray(POOL_WINDOWS, np.float32), POOL_GROUP)[None, :]
    cnt = jnp.asarray(np.minimum(np.arange(1, HIST_ROWS + 1, dtype=np.float32)[:, None], win))
    weights = (w_in[0].astype(bf), row(ln_v_g[0]), row(ln_v_b[0]), ws_masked, bs_tbl, pool_w[0].astype(bf),
               row(pool_scale[0]), w_out[0].astype(bf), row(b_out[0]), row(ln_g[0]), row(ln_b[0]),
               w_ple[0].astype(bf), w_pg[0].astype(bf), row(b_pg[0]))

    n_inner = S // PROMPT_TILE_ROWS
    cfg_p = TileConfig(n_sub=1, sub_len=PROMPT_TILE_ROWS, n_inner=n_inner, n_tiles=B * n_inner,
                       sequential=True, pipelined=True, emit_v=False)
    hist_p = jnp.zeros((B, HIST_ROWS, D_B), jnp.float32)
    y_p, hist_p_new = _run_layer(cfg_p, x_prompt, p_prompt[0], hist_p, cnt, weights)

    assert PAST_LEN >= POOL_HIST
    ns = SAMPLE_STREAMS_PER_TILE
    cfg_s = TileConfig(n_sub=ns, sub_len=Ss, n_inner=1, n_tiles=Bs // ns,
                       sequential=False, pipelined=False, emit_v=True)
    hist_s = jnp.pad(state_pool[0], ((0, 0), (1, 0), (0, 0)))
    y_s, hist_s_new, v_s = _run_layer(
        cfg_s, x_sample.reshape(Bs // ns, ns * Ss, D_MODEL),
        p_sample[0].reshape(Bs // ns, ns * Ss, D_PLE), hist_s, cnt, weights)

    return (y_p,
            y_s.reshape(Bs, Ss, D_MODEL),
            hist_p_new[None, :, 1:, :],
            hist_s_new[None, :, 1:, :],
            v_s.reshape(1, Bs, Ss, N_HEADS_A, HEAD_A))
```

```python
import functools
from typing import NamedTuple

import numpy as np
import jax
import jax.numpy as jnp
from jax.experimental import pallas as pl
from jax.experimental.pallas import tpu as pltpu

D_MODEL = 1024
DEPTH = 1
PAST_LEN = 4096
CHUNK = 64
MLP_CHUNK = 128
D_MIX = 2 * D_MODEL
D_A = D_MIX // 2
HEAD_A = 128
N_HEADS_A = D_A // HEAD_A
D_B = D_MIX - D_A
POOL_WINDOWS = (2, 4, 8, 16)
N_POOL_GROUPS = len(POOL_WINDOWS)
POOL_GROUP = D_B // N_POOL_GROUPS
POOL_HIST = max(POOL_WINDOWS) - 1
HIST_ROWS = POOL_HIST + 1
D_PLE = 256
D_IN = 3 * D_A + 2 * D_B
ALPHA = (2.0 * DEPTH) ** 0.25
LN_EPS = 1e-5

PROMPT_TILE_ROWS = 512
SAMPLE_STREAMS_PER_TILE = 8
VMEM_LIMIT_BYTES = 56 * 1024 * 1024
COL_PIECE = 256
N_COL_PIECES = D_MODEL // COL_PIECE
assert COL_PIECE == POOL_GROUP and COL_PIECE == 2 * HEAD_A


def _numbered(name, n):
    return tuple(f"{name}{i}" for i in range(n))


GA_DOTS = 1
GATE_DOTS = 1
STAGE1_PIECES = (("v_proj", "u_proj") + _numbered("ga_proj", GA_DOTS) + _numbered("ln_v", N_HEADS_A)
                 + _numbered("a_out", N_COL_PIECES) + ("hp_proj", "gb_proj") + _numbered("pooling", N_POOL_GROUPS)
                 + _numbered("b_out", N_POOL_GROUPS))
STAGE2_PIECES = ("w_out", "post_norm") + _numbered("gate", GATE_DOTS)
PIPELINE_ORDER = (
    "v_proj", "w_out", "u_proj", "ga_proj0",
    "ln_v0", "ln_v1", "ln_v2", "ln_v3", "ln_v4", "ln_v5", "ln_v6", "ln_v7",
    "post_norm", "a_out0", "a_out1", "a_out2", "a_out3", "hp_proj",
    "gate0", "gb_proj",
    "pooling0", "pooling1", "pooling2", "pooling3", "b_out0", "b_out1", "b_out2", "b_out3")
assert sorted(PIPELINE_ORDER) == sorted(STAGE1_PIECES + STAGE2_PIECES)


class TileConfig(NamedTuple):
    n_sub: int
    sub_len: int
    n_inner: int
    n_tiles: int
    sequential: bool
    pipelined: bool
    emit_v: bool

    @property
    def tile_rows(self):
        return self.n_sub * self.sub_len


def _layernorm(x, g, b):
    mu = jnp.mean(x, axis=-1, keepdims=True)
    xc = x - mu
    var = jnp.mean(xc * xc, axis=-1, keepdims=True)
    return xc * jax.lax.rsqrt(var + LN_EPS) * g + b


def _dot(a, b):
    return jnp.dot(a, b, preferred_element_type=jnp.float32)


def _layer_kernel(cfg, x_ref, xres_ref, p_ref, hist_ref, cnt_ref, w_in_ref, lnv_g_ref, lnv_b_ref, ws_ref, bs_ref,
                  pool_w_ref, pool_scale_ref, w_out_ref, b_out_ref, ln_g_ref, ln_b_ref, w_ple_ref,
                  w_pg_ref, b_pg_ref, *out_and_scratch):
    if cfg.emit_v:
        y_ref, hist_out_ref, v_out_ref, z_ref, mix_ref, cat_ref = out_and_scratch
    else:
        y_ref, hist_out_ref, z_ref, mix_ref, cat_ref = out_and_scratch
        v_out_ref = None
    L = cfg.sub_len
    last_sub = cfg.n_sub - 1
    step = pl.program_id(0)
    j = jax.lax.rem(jnp.minimum(step, cfg.n_tiles - 1), cfg.n_inner)

    if cfg.sequential:
        @pl.when(j == 0)
        def _():
            z_ref[last_sub, L:, :] = hist_ref[0]

    if cfg.pipelined:
        @pl.when(step == 0)
        def _():
            cat_ref[...] = jnp.zeros_like(cat_ref)

    val = {}
    pieces = {}

    def piece(name):
        def register(fn):
            pieces[name] = fn
        return register

    def col_piece(n, base=0):
        return slice(base + n * COL_PIECE, base + (n + 1) * COL_PIECE)

    def proj(lo, hi):
        return _dot(val["xb"], w_in_ref[:, lo:hi])

    @piece("v_proj")
    def _():
        val["xb"] = x_ref[0].astype(jnp.bfloat16)
        val["v"] = proj(D_A, 2 * D_A)

    @piece("u_proj")
    def _():
        val["u"] = proj(0, D_A)

    ga_width = D_A // GA_DOTS
    for n in range(GA_DOTS):
        @piece(f"ga_proj{n}")
        def _(n=n):
            val[f"ga{n}"] = proj(2 * D_A + n * ga_width, 2 * D_A + (n + 1) * ga_width)

    @piece("hp_proj")
    def _():
        hp = val["hp"] = proj(3 * D_A, 3 * D_A + D_B)
        if cfg.sequential:
            z_ref[0, :HIST_ROWS, :] = z_ref[last_sub, L:, :]
        else:
            z_ref[:, :HIST_ROWS, :] = hist_ref[...]
        for s in range(cfg.n_sub):
            if cfg.sequential and s > 0:
                z_ref[s, :HIST_ROWS, :] = z_ref[s - 1, L:, :]
            z_ref[s, HIST_ROWS:, :] = hp[s * L:(s + 1) * L]
        if cfg.sequential:
            hist_out_ref[0] = z_ref[last_sub, L:, :]
        else:
            for s in range(cfg.n_sub):
                hist_out_ref[s] = z_ref[s, L:, :]

    @piece("gb_proj")
    def _():
        val["gb"] = proj(3 * D_A + D_B, D_IN)

    for h in range(N_HEADS_A):
        @piece(f"ln_v{h}")
        def _(h=h):
            sl = slice(h * HEAD_A, (h + 1) * HEAD_A)
            vn = _layernorm(val["v"][:, sl], lnv_g_ref[:, sl], lnv_b_ref[:, sl])
            if v_out_ref is not None:
                v_out_ref[0, :, sl] = vn
            vnb = vn.astype(jnp.bfloat16)
            for s in range(cfg.n_sub):
                for c0 in range(0, L, MLP_CHUNK):
                    lc = min(MLP_CHUNK, L - c0)
                    r0 = s * L + c0
                    mixed = _dot(ws_ref[h, :lc, :lc], vnb[r0:r0 + lc])
                    mix_ref[r0:r0 + lc, sl] = (mixed + bs_ref[:lc, sl]).astype(jnp.bfloat16)

    for n in range(N_COL_PIECES):
        @piece(f"a_out{n}")
        def _(n=n):
            cols = col_piece(n)
            k, off = divmod(n * COL_PIECE, ga_width)
            ga = val[f"ga{k}"][:, off:off + COL_PIECE]
            a_out = val["u"][:, cols] * mix_ref[:, cols].astype(jnp.float32) * jax.nn.silu(ga)
            cat_ref[:, cols] = a_out.astype(jnp.bfloat16)

    for g, w in enumerate(POOL_WINDOWS):
        @piece(f"pooling{g}")
        def _(g=g, w=w):
            cols, ccols = col_piece(g), col_piece(g, D_A)
            for s in range(cfg.n_sub):
                rows = slice(s * L, (s + 1) * L)
                acc = z_ref[s, :, cols]
                span = 1
                while span < w:
                    acc = acc + pltpu.roll(acc, span, axis=0)
                    span *= 2
                win = acc[HIST_ROWS:]
                hp_g = val["hp"][rows, cols]
                mix_ref[rows, ccols] = (win * (1.0 / w) - hp_g).astype(jnp.bfloat16)
                if cfg.sequential and s == 0:
                    cnt = jnp.where(j == 0, cnt_ref[:, cols], float(w))
                    head = win[:HIST_ROWS] / cnt - hp_g[:HIST_ROWS]
                    mix_ref[:HIST_ROWS, ccols] = head.astype(jnp.bfloat16)

        @piece(f"b_out{g}")
        def _(g=g):
            cols, ccols = col_piece(g), col_piece(g, D_A)
            q = _dot(mix_ref[:, ccols], pool_w_ref[g])
            b_outp = q * pool_scale_ref[:, cols] * jax.nn.silu(val["gb"][:, cols])
            cat_ref[:, ccols] = b_outp.astype(jnp.bfloat16)

    @piece("w_out")
    def _():
        y = _dot(cat_ref[...], w_out_ref[...]) + b_out_ref[...]
        val["pre"] = ALPHA * xres_ref[0] + y

    @piece("post_norm")
    def _():
        xn = val["xn"] = _layernorm(val["pre"], ln_g_ref[...], ln_b_ref[...])
        val["xnb"] = xn.astype(jnp.bfloat16)
        val["pb"] = p_ref[0].astype(jnp.bfloat16)

    gate_width = D_MODEL // GATE_DOTS
    for n in range(GATE_DOTS):
        @piece(f"gate{n}")
        def _(n=n):
            cols = slice(n * gate_width, (n + 1) * gate_width)
            gate = jax.nn.sigmoid(_dot(val["xnb"], w_pg_ref[:, cols]) + b_pg_ref[:, cols])
            ple = _dot(val["pb"], w_ple_ref[:, cols])
            y_ref[0, :, cols] = val["xn"][:, cols] + gate * ple

    assert set(pieces) == set(STAGE1_PIECES + STAGE2_PIECES)
    for name in (PIPELINE_ORDER if cfg.pipelined else STAGE1_PIECES + STAGE2_PIECES):
        pieces[name]()


def _resident(shape):
    nd = len(shape)
    return pl.BlockSpec(shape, lambda i: (0,) * nd, pipeline_mode=pl.Buffered(1))


def _run_layer(cfg, x, p, hist, cnt, weights):
    T = cfg.tile_rows
    n_seq = cfg.n_tiles // cfg.n_inner
    n_hist = 1 if cfg.sequential else cfg.n_sub
    assert x.shape == (n_seq, cfg.n_inner * T, D_MODEL) and p.shape == (n_seq, cfg.n_inner * T, D_PLE)
    assert hist.shape == (n_seq * n_hist, HIST_ROWS, D_B)
    assert cfg.sequential or cfg.n_inner == 1

    def cur(i):
        t = jnp.minimum(i, cfg.n_tiles - 1)
        return t // cfg.n_inner, t % cfg.n_inner

    def res(i):
        t = jnp.maximum(i - 1, 0) if cfg.pipelined else i
        return t // cfg.n_inner, t % cfg.n_inner

    in_specs = [
        pl.BlockSpec((1, T, D_MODEL), lambda i: (*cur(i), 0)),
        pl.BlockSpec((1, T, D_MODEL), lambda i: (*res(i), 0)),
        pl.BlockSpec((1, T, D_PLE), lambda i: (*res(i), 0)),
        pl.BlockSpec((n_hist, HIST_ROWS, D_B), lambda i: (cur(i)[0], 0, 0)),
        _resident(cnt.shape),
    ] + [_resident(w.shape) for w in weights]
    out_shape = [jax.ShapeDtypeStruct(x.shape, jnp.float32),
                 jax.ShapeDtypeStruct(hist.shape, jnp.float32)]
    out_specs = [pl.BlockSpec((1, T, D_MODEL), lambda i: (*res(i), 0)),
                 pl.BlockSpec((n_hist, HIST_ROWS, D_B), lambda i: (cur(i)[0], 0, 0))]
    if cfg.emit_v:
        out_shape.append(jax.ShapeDtypeStruct(x.shape[:2] + (D_A,), jnp.float32))
        out_specs.append(pl.BlockSpec((1, T, D_A), lambda i: (*cur(i), 0)))
    return pl.pallas_call(
        functools.partial(_layer_kernel, cfg),
        grid=(cfg.n_tiles + (1 if cfg.pipelined else 0),),
        in_specs=in_specs,
        out_specs=out_specs,
        out_shape=out_shape,
        scratch_shapes=[pltpu.VMEM((cfg.n_sub, HIST_ROWS + cfg.sub_len, D_B), jnp.float32),
                        pltpu.VMEM((T, D_MIX), jnp.bfloat16),
                        pltpu.VMEM((T, D_MIX), jnp.bfloat16)],
        compiler_params=pltpu.CompilerParams(dimension_semantics=("arbitrary",),
                                             vmem_limit_bytes=VMEM_LIMIT_BYTES),
        name="layer_sample" if cfg.emit_v else "layer_prompt",
    )(x, x, p, hist, cnt, *weights)


def kernel(x_prompt, x_sample, state_pool, p_prompt, p_sample, w_in, ln_v_g, ln_v_b, w_s, b_s, pool_w, pool_scale, w_out, b_out, ln_g, ln_b, w_ple, w_pg, b_pg):
    assert DEPTH == 1 and w_in.shape[0] == DEPTH
    B, S, _ = x_prompt.shape
    Bs, Ss, _ = x_sample.shape
    bf = jnp.bfloat16
    row = lambda a: a.reshape(1, -1).astype(jnp.float32)

    blk = np.arange(MLP_CHUNK) // CHUNK
    mask = jnp.asarray(blk[None, :] <= blk[:, None])
    ws_masked = jnp.where(mask[None], w_s[0], jnp.zeros((), w_s.dtype)).astype(bf)
    bs_tbl = jnp.repeat(jnp.transpose(b_s[0]), HEAD_A, axis=1).astype(jnp.float32)
    win = np.repeat(np.asarray(POOL_WINDOWS, np.float32), POOL_GROUP)[None, :]
    cnt = jnp.asarray(np.minimum(np.arange(1, HIST_ROWS + 1, dtype=np.float32)[:, None], win))
    weights = (w_in[0].astype(bf), row(ln_v_g[0]), row(ln_v_b[0]), ws_masked, bs_tbl, pool_w[0].astype(bf),
               row(pool_scale[0]), w_out[0].astype(bf), row(b_out[0]), row(ln_g[0]), row(ln_b[0]),
               w_ple[0].astype(bf), w_pg[0].astype(bf), row(b_pg[0]))

    n_inner = S // PROMPT_TILE_ROWS
    cfg_p = TileConfig(n_sub=1, sub_len=PROMPT_TILE_ROWS, n_inner=n_inner, n_tiles=B * n_inner,
                       sequential=True, pipelined=True, emit_v=False)
    hist_p = jnp.zeros((B, HIST_ROWS, D_B), jnp.float32)
    y_p, hist_p_new = _run_layer(cfg_p, x_prompt, p_prompt[0], hist_p, cnt, weights)

    assert PAST_LEN >= POOL_HIST
    ns = SAMPLE_STREAMS_PER_TILE
    cfg_s = TileConfig(n_sub=ns, sub_len=Ss, n_inner=1, n_tiles=Bs // ns,
                       sequential=False, pipelined=False, emit_v=True)
    hist_s = jnp.pad(state_pool[0], ((0, 0), (1, 0), (0, 0)))
    y_s, hist_s_new, v_s = _run_layer(
        cfg_s, x_sample.reshape(Bs // ns, ns * Ss, D_MODEL),
        p_sample[0].reshape(Bs // ns, ns * Ss, D_PLE), hist_s, cnt, weights)

    return (y_p,
            y_s.reshape(Bs, Ss, D_MODEL),
            hist_p_new[None, :, 1:, :],
            hist_s_new[None, :, 1:, :],
            v_s.reshape(1, Bs, Ss, N_HEADS_A, HEAD_A))
```

```python
import functools
from typing import NamedTuple

import numpy as np
import jax
import jax.numpy as jnp
from jax.experimental import pallas as pl
from jax.experimental.pallas import tpu as pltpu

D_MODEL = 1024
DEPTH = 1
PAST_LEN = 4096
CHUNK = 64
MLP_CHUNK = 128
D_MIX = 2 * D_MODEL
D_A = D_MIX // 2
HEAD_A = 128
N_HEADS_A = D_A // HEAD_A
D_B = D_MIX - D_A
POOL_WINDOWS = (2, 4, 8, 16)
N_POOL_GROUPS = len(POOL_WINDOWS)
POOL_GROUP = D_B // N_POOL_GROUPS
POOL_HIST = max(POOL_WINDOWS) - 1
HIST_ROWS = POOL_HIST + 1
D_PLE = 256
D_IN = 3 * D_A + 2 * D_B
ALPHA = (2.0 * DEPTH) ** 0.25
LN_EPS = 1e-5

PROMPT_TILE_ROWS = 512
SAMPLE_STREAMS_PER_TILE = 8
VMEM_LIMIT_BYTES = 56 * 1024 * 1024
COL_PIECE = 256
N_COL_PIECES = D_MODEL // COL_PIECE
assert COL_PIECE == POOL_GROUP and COL_PIECE == 2 * HEAD_A


def _numbered(name, n):
    return tuple(f"{name}{i}" for i in range(n))


GA_DOTS = N_COL_PIECES
GATE_DOTS = N_COL_PIECES
STAGE1_PIECES = (("v_proj", "u_proj") + _numbered("ga_proj", GA_DOTS) + _numbered("ln_v", N_HEADS_A)
                 + _numbered("a_out", N_COL_PIECES) + ("hp_proj", "gb_proj") + _numbered("pooling", N_POOL_GROUPS)
                 + _numbered("b_out", N_POOL_GROUPS))
STAGE2_PIECES = ("w_out", "post_norm") + _numbered("gate", GATE_DOTS)
PIPELINE_ORDER = (
    "v_proj", "w_out", "u_proj",
    "ln_v0", "ln_v1", "ga_proj0", "ln_v2", "ln_v3", "ga_proj1",
    "ln_v4", "ln_v5", "ga_proj2", "ln_v6", "ln_v7", "ga_proj3",
    "post_norm", "a_out0", "a_out1", "a_out2", "a_out3", "hp_proj",
    "gate0", "gate1", "gate2", "gate3", "gb_proj",
    "pooling0", "pooling1", "pooling2", "pooling3", "b_out0", "b_out1", "b_out2", "b_out3")
assert sorted(PIPELINE_ORDER) == sorted(STAGE1_PIECES + STAGE2_PIECES)


class TileConfig(NamedTuple):
    n_sub: int
    sub_len: int
    n_inner: int
    n_tiles: int
    sequential: bool
    pipelined: bool
    emit_v: bool

    @property
    def tile_rows(self):
        return self.n_sub * self.sub_len


def _layernorm(x, g, b):
    mu = jnp.mean(x, axis=-1, keepdims=True)
    xc = x - mu
    var = jnp.mean(xc * xc, axis=-1, keepdims=True)
    return xc * jax.lax.rsqrt(var + LN_EPS) * g + b


def _dot(a, b):
    return jnp.dot(a, b, preferred_element_type=jnp.float32)


def _layer_kernel(cfg, x_ref, xres_ref, p_ref, hist_ref, cnt_ref, w_in_ref, lnv_g_ref, lnv_b_ref, ws_ref, bs_ref,
                  pool_w_ref, pool_scale_ref, w_out_ref, b_out_ref, ln_g_ref, ln_b_ref, w_ple_ref,
                  w_pg_ref, b_pg_ref, *out_and_scratch):
    if cfg.emit_v:
        y_ref, hist_out_ref, v_out_ref, z_ref, mix_ref, cat_ref = out_and_scratch
    else:
        y_ref, hist_out_ref, z_ref, mix_ref, cat_ref = out_and_scratch
        v_out_ref = None
    L = cfg.sub_len
    last_sub = cfg.n_sub - 1
    step = pl.program_id(0)
    j = jax.lax.rem(jnp.minimum(step, cfg.n_tiles - 1), cfg.n_inner)

    if cfg.sequential:
        @pl.when(j == 0)
        def _():
            z_ref[last_sub, L:L + 1, :] = jnp.zeros((1, D_B), jnp.float32)
            z_ref[last_sub, L + 1:, :] = hist_ref[0]

    if cfg.pipelined:
        @pl.when(step == 0)
        def _():
            cat_ref[...] = jnp.zeros_like(cat_ref)

    val = {}
    pieces = {}

    def piece(name):
        def register(fn):
            pieces[name] = fn
        return register

    def col_piece(n, base=0):
        return slice(base + n * COL_PIECE, base + (n + 1) * COL_PIECE)

    def proj(lo, hi):
        return _dot(val["xb"], w_in_ref[:, lo:hi])

    @piece("v_proj")
    def _():
        val["xb"] = x_ref[0].astype(jnp.bfloat16)
        val["v"] = proj(D_A, 2 * D_A)

    @piece("u_proj")
    def _():
        val["u"] = proj(0, D_A)

    ga_width = D_A // GA_DOTS
    for n in range(GA_DOTS):
        @piece(f"ga_proj{n}")
        def _(n=n):
            val[f"ga{n}"] = proj(2 * D_A + n * ga_width, 2 * D_A + (n + 1) * ga_width)

    @piece("hp_proj")
    def _():
        hp = val["hp"] = proj(3 * D_A, 3 * D_A + D_B)
        if cfg.sequential:
            z_ref[0, :HIST_ROWS, :] = z_ref[last_sub, L:, :]
        else:
            z_ref[:, :1, :] = jnp.zeros((cfg.n_sub, 1, D_B), jnp.float32)
            z_ref[:, 1:HIST_ROWS, :] = hist_ref[...]
        for s in range(cfg.n_sub):
            if cfg.sequential and s > 0:
                z_ref[s, :HIST_ROWS, :] = z_ref[s - 1, L:, :]
            z_ref[s, HIST_ROWS:, :] = hp[s * L:(s + 1) * L]
        if cfg.sequential:
            hist_out_ref[0] = z_ref[last_sub, L + 1:, :]
        else:
            for s in range(cfg.n_sub):
                hist_out_ref[s] = z_ref[s, L + 1:, :]

    @piece("gb_proj")
    def _():
        val["gb"] = proj(3 * D_A + D_B, D_IN)

    for h in range(N_HEADS_A):
        @piece(f"ln_v{h}")
        def _(h=h):
            sl = slice(h * HEAD_A, (h + 1) * HEAD_A)
            vn = _layernorm(val["v"][:, sl], lnv_g_ref[:, sl], lnv_b_ref[:, sl])
            if v_out_ref is not None:
                v_out_ref[0, :, sl] = vn
            vnb = vn.astype(jnp.bfloat16)
            for s in range(cfg.n_sub):
                for c0 in range(0, L, MLP_CHUNK):
                    lc = min(MLP_CHUNK, L - c0)
                    r0 = s * L + c0
                    mixed = _dot(ws_ref[h, :lc, :lc], vnb[r0:r0 + lc])
                    mix_ref[r0:r0 + lc, sl] = (mixed + bs_ref[:lc, sl]).astype(jnp.bfloat16)

    for n in range(N_COL_PIECES):
        @piece(f"a_out{n}")
        def _(n=n):
            cols = col_piece(n)
            k, off = divmod(n * COL_PIECE, ga_width)
            ga = val[f"ga{k}"][:, off:off + COL_PIECE]
            a_out = val["u"][:, cols] * mix_ref[:, cols].astype(jnp.float32) * jax.nn.silu(ga)
            cat_ref[:, cols] = a_out.astype(jnp.bfloat16)

    for g, w in enumerate(POOL_WINDOWS):
        @piece(f"pooling{g}")
        def _(g=g, w=w):
            cols, ccols = col_piece(g), col_piece(g, D_A)
            for s in range(cfg.n_sub):
                rows = slice(s * L, (s + 1) * L)
                acc = z_ref[s, :, cols]
                span = 1
                while span < w:
                    acc = acc + pltpu.roll(acc, span, axis=0)
                    span *= 2
                win = acc[HIST_ROWS:]
                hp_g = val["hp"][rows, cols]
                mix_ref[rows, ccols] = (win * (1.0 / w) - hp_g).astype(jnp.bfloat16)
                if cfg.sequential and s == 0:
                    cnt = jnp.where(j == 0, cnt_ref[:, cols], float(w))
                    head = win[:HIST_ROWS] / cnt - hp_g[:HIST_ROWS]
                    mix_ref[:HIST_ROWS, ccols] = head.astype(jnp.bfloat16)

        @piece(f"b_out{g}")
        def _(g=g):
            cols, ccols = col_piece(g), col_piece(g, D_A)
            q = _dot(mix_ref[:, ccols], pool_w_ref[g])
            b_outp = q * pool_scale_ref[:, cols] * jax.nn.silu(val["gb"][:, cols])
            cat_ref[:, ccols] = b_outp.astype(jnp.bfloat16)

    @piece("w_out")
    def _():
        y = _dot(cat_ref[...], w_out_ref[...]) + b_out_ref[...]
        val["pre"] = ALPHA * xres_ref[0] + y

    @piece("post_norm")
    def _():
        xn = val["xn"] = _layernorm(val["pre"], ln_g_ref[...], ln_b_ref[...])
        val["xnb"] = xn.astype(jnp.bfloat16)
        val["pb"] = p_ref[0].astype(jnp.bfloat16)

    gate_width = D_MODEL // GATE_DOTS
    for n in range(GATE_DOTS):
        @piece(f"gate{n}")
        def _(n=n):
            cols = slice(n * gate_width, (n + 1) * gate_width)
            gate = jax.nn.sigmoid(_dot(val["xnb"], w_pg_ref[:, cols]) + b_pg_ref[:, cols])
            ple = _dot(val["pb"], w_ple_ref[:, cols])
            y_ref[0, :, cols] = val["xn"][:, cols] + gate * ple

    assert set(pieces) == set(STAGE1_PIECES + STAGE2_PIECES)
    for name in (PIPELINE_ORDER if cfg.pipelined else STAGE1_PIECES + STAGE2_PIECES):
        pieces[name]()


def _resident(shape):
    nd = len(shape)
    return pl.BlockSpec(shape, lambda i: (0,) * nd, pipeline_mode=pl.Buffered(1))


def _run_layer(cfg, x, p, hist, cnt, weights):
    T = cfg.tile_rows
    n_seq = cfg.n_tiles // cfg.n_inner
    n_hist = 1 if cfg.sequential else cfg.n_sub
    assert x.shape == (n_seq, cfg.n_inner * T, D_MODEL) and p.shape == (n_seq, cfg.n_inner * T, D_PLE)
    assert hist.shape == ((1 if cfg.sequential else n_seq * n_hist), POOL_HIST, D_B)
    assert cfg.sequential or cfg.n_inner == 1

    def cur(i):
        t = jnp.minimum(i, cfg.n_tiles - 1)
        return t // cfg.n_inner, t % cfg.n_inner

    def res(i):
        t = jnp.maximum(i - 1, 0) if cfg.pipelined else i
        return t // cfg.n_inner, t % cfg.n_inner

    in_specs = [
        pl.BlockSpec((1, T, D_MODEL), lambda i: (*cur(i), 0)),
        pl.BlockSpec((1, T, D_MODEL), lambda i: (*res(i), 0)),
        pl.BlockSpec((1, T, D_PLE), lambda i: (*res(i), 0)),
        pl.BlockSpec((n_hist, POOL_HIST, D_B), lambda i: ((0 if cfg.sequential else cur(i)[0]), 0, 0)),
        _resident(cnt.shape),
    ] + [_resident(w.shape) for w in weights]
    out_shape = [jax.ShapeDtypeStruct(x.shape, jnp.float32),
                 jax.ShapeDtypeStruct((n_seq * n_hist, POOL_HIST, D_B), jnp.float32)]
    out_specs = [pl.BlockSpec((1, T, D_MODEL), lambda i: (*res(i), 0)),
                 pl.BlockSpec((n_hist, POOL_HIST, D_B), lambda i: (cur(i)[0], 0, 0))]
    if cfg.emit_v:
        out_shape.append(jax.ShapeDtypeStruct(x.shape[:2] + (D_A,), jnp.float32))
        out_specs.append(pl.BlockSpec((1, T, D_A), lambda i: (*cur(i), 0)))
    return pl.pallas_call(
        functools.partial(_layer_kernel, cfg),
        grid=(cfg.n_tiles + (1 if cfg.pipelined else 0),),
        in_specs=in_specs,
        out_specs=out_specs,
        out_shape=out_shape,
        scratch_shapes=[pltpu.VMEM((cfg.n_sub, HIST_ROWS + cfg.sub_len, D_B), jnp.float32),
                        pltpu.VMEM((T, D_MIX), jnp.bfloat16),
                        pltpu.VMEM((T, D_MIX), jnp.bfloat16)],
        compiler_params=pltpu.CompilerParams(dimension_semantics=("arbitrary",),
                                             vmem_limit_bytes=VMEM_LIMIT_BYTES),
        name="layer_sample" if cfg.emit_v else "layer_prompt",
    )(x, x, p, hist, cnt, *weights)


def kernel(x_prompt, x_sample, state_pool, p_prompt, p_sample, w_in, ln_v_g, ln_v_b, w_s, b_s, pool_w, pool_scale, w_out, b_out, ln_g, ln_b, w_ple, w_pg, b_pg):
    assert DEPTH == 1 and w_in.shape[0] == DEPTH
    B, S, _ = x_prompt.shape
    Bs, Ss, _ = x_sample.shape
    bf = jnp.bfloat16
    row = lambda a: a.reshape(1, -1).astype(jnp.float32)

    blk = np.arange(MLP_CHUNK) // CHUNK
    mask = jnp.asarray(blk[None, :] <= blk[:, None])
    ws_masked = jnp.where(mask[None], w_s[0], jnp.zeros((), w_s.dtype)).astype(bf)
    bs_tbl = jnp.repeat(jnp.transpose(b_s[0]), HEAD_A, axis=1).astype(jnp.float32)
    win = np.repeat(np.asarray(POOL_WINDOWS, np.float32), POOL_GROUP)[None, :]
    cnt = jnp.asarray(np.minimum(np.arange(1, HIST_ROWS + 1, dtype=np.float32)[:, None], win))
    weights = (w_in[0].astype(bf), row(ln_v_g[0]), row(ln_v_b[0]), ws_masked, bs_tbl, pool_w[0].astype(bf),
               row(pool_scale[0]), w_out[0].astype(bf), row(b_out[0]), row(ln_g[0]), row(ln_b[0]),
               w_ple[0].astype(bf), w_pg[0].astype(bf), row(b_pg[0]))

    n_inner = S // PROMPT_TILE_ROWS
    cfg_p = TileConfig(n_sub=1, sub_len=PROMPT_TILE_ROWS, n_inner=n_inner, n_tiles=B * n_inner,
                       sequential=True, pipelined=True, emit_v=False)
    hist_p = jnp.zeros((1, POOL_HIST, D_B), jnp.float32)
    y_p, hist_p_new = _run_layer(cfg_p, x_prompt, p_prompt[0], hist_p, cnt, weights)

    assert PAST_LEN >= POOL_HIST
    ns = SAMPLE_STREAMS_PER_TILE
    cfg_s = TileConfig(n_sub=ns, sub_len=Ss, n_inner=1, n_tiles=Bs // ns,
                       sequential=False, pipelined=False, emit_v=True)
    hist_s = state_pool[0]
    y_s, hist_s_new, v_s = _run_layer(
        cfg_s, x_sample.reshape(Bs // ns, ns * Ss, D_MODEL),
        p_sample[0].reshape(Bs // ns, ns * Ss, D_PLE), hist_s, cnt, weights)

    return (y_p,
            y_s.reshape(Bs, Ss, D_MODEL),
            hist_p_new[None],
            hist_s_new[None],
            v_s.reshape(1, Bs, Ss, N_HEADS_A, HEAD_A))
```

```python
import functools
from typing import NamedTuple

import numpy as np
import jax
import jax.numpy as jnp
from jax.experimental import pallas as pl
from jax.experimental.pallas import tpu as pltpu

D_MODEL = 1024
DEPTH = 1
PAST_LEN = 4096
CHUNK = 64
MLP_CHUNK = 128
D_MIX = 2 * D_MODEL
D_A = D_MIX // 2
HEAD_A = 128
N_HEADS_A = D_A // HEAD_A
D_B = D_MIX - D_A
POOL_WINDOWS = (2, 4, 8, 16)
N_POOL_GROUPS = len(POOL_WINDOWS)
POOL_GROUP = D_B // N_POOL_GROUPS
POOL_HIST = max(POOL_WINDOWS) - 1
HIST_ROWS = POOL_HIST + 1
D_PLE = 256
D_IN = 3 * D_A + 2 * D_B
ALPHA = (2.0 * DEPTH) ** 0.25
LN_EPS = 1e-5

PROMPT_TILE_ROWS = 512
SAMPLE_STREAMS_PER_TILE = 8
VMEM_LIMIT_BYTES = 56 * 1024 * 1024
COL_PIECE = 256
N_COL_PIECES = D_MODEL // COL_PIECE
assert COL_PIECE == POOL_GROUP and COL_PIECE == 2 * HEAD_A


def _numbered(name, n):
    return tuple(f"{name}{i}" for i in range(n))


GA_DOTS = N_COL_PIECES
GATE_DOTS = N_COL_PIECES
STAGE1_PIECES = (("v_proj", "u_proj") + _numbered("ga_proj", GA_DOTS) + _numbered("ln_v", N_HEADS_A)
                 + _numbered("a_out", N_COL_PIECES) + ("hp_proj", "gb_proj") + _numbered("pooling", N_POOL_GROUPS)
                 + _numbered("b_out", N_POOL_GROUPS))
STAGE2_PIECES = ("w_out", "post_norm") + _numbered("gate", GATE_DOTS)
PIPELINE_ORDER = (
    "v_proj", "w_out", "u_proj",
    "ln_v0", "ln_v1", "ga_proj0", "ln_v2", "ln_v3", "ga_proj1",
    "ln_v4", "ln_v5", "ga_proj2", "ln_v6", "ln_v7", "ga_proj3",
    "post_norm", "a_out0", "a_out1", "a_out2", "a_out3", "hp_proj",
    "gate0", "gate1", "gate2", "gate3", "gb_proj",
    "pooling0", "pooling1", "pooling2", "pooling3", "b_out0", "b_out1", "b_out2", "b_out3")
assert sorted(PIPELINE_ORDER) == sorted(STAGE1_PIECES + STAGE2_PIECES)


class TileConfig(NamedTuple):
    n_sub: int
    sub_len: int
    n_inner: int
    n_tiles: int
    sequential: bool
    pipelined: bool
    emit_v: bool

    @property
    def tile_rows(self):
        return self.n_sub * self.sub_len


def _layernorm(x, g, b):
    mu = jnp.mean(x, axis=-1, keepdims=True)
    xc = x - mu
    var = jnp.mean(xc * xc, axis=-1, keepdims=True)
    return xc * jax.lax.rsqrt(var + LN_EPS) * g + b


def _dot(a, b):
    return jnp.dot(a, b, preferred_element_type=jnp.float32)


def _layer_kernel(cfg, x_ref, xres_ref, p_ref, hist_ref, cnt_ref, w_in_ref, lnv_g_ref, lnv_b_ref, ws_ref, bs_ref,
                  pool_w_ref, pool_scale_ref, w_out_ref, b_out_ref, ln_g_ref, ln_b_ref, w_ple_ref,
                  w_pg_ref, b_pg_ref, *out_and_scratch):
    if cfg.emit_v:
        y_ref, hist_out_ref, v_out_ref, z_ref, mix_ref, cat_ref = out_and_scratch
    else:
        y_ref, hist_out_ref, z_ref, mix_ref, cat_ref = out_and_scratch
        v_out_ref = None
    L = cfg.sub_len
    last_sub = cfg.n_sub - 1
    step = pl.program_id(0)
    j = jax.lax.rem(jnp.minimum(step, cfg.n_tiles - 1), cfg.n_inner)

    if cfg.sequential:
        @pl.when(j == 0)
        def _():
            z_ref[last_sub, L:L + 1, :] = jnp.zeros((1, D_B), jnp.float32)
            z_ref[last_sub, L + 1:, :] = hist_ref[0]

    if cfg.pipelined:
        @pl.when(step == 0)
        def _():
            cat_ref[...] = jnp.zeros_like(cat_ref)

    val = {}
    pieces = {}

    def piece(name):
        def register(fn):
            pieces[name] = fn
        return register

    def col_piece(n, base=0):
        return slice(base + n * COL_PIECE, base + (n + 1) * COL_PIECE)

    def proj(lo, hi):
        return _dot(val["xb"], w_in_ref[:, lo:hi])

    @piece("v_proj")
    def _():
        val["xb"] = x_ref[0].astype(jnp.bfloat16)
        val["v"] = proj(D_A, 2 * D_A)

    @piece("u_proj")
    def _():
        val["u"] = proj(0, D_A)

    ga_width = D_A // GA_DOTS
    for n in range(GA_DOTS):
        @piece(f"ga_proj{n}")
        def _(n=n):
            val[f"ga{n}"] = proj(2 * D_A + n * ga_width, 2 * D_A + (n + 1) * ga_width)

    @piece("hp_proj")
    def _():
        hp = val["hp"] = proj(3 * D_A, 3 * D_A + D_B)
        if cfg.sequential:
            z_ref[0, :HIST_ROWS, :] = z_ref[last_sub, L:, :]
        else:
            z_ref[:, :1, :] = jnp.zeros((cfg.n_sub, 1, D_B), jnp.float32)
            z_ref[:, 1:HIST_ROWS, :] = hist_ref[...]
        for s in range(cfg.n_sub):
            if cfg.sequential and s > 0:
                z_ref[s, :HIST_ROWS, :] = z_ref[s - 1, L:, :]
            z_ref[s, HIST_ROWS:, :] = hp[s * L:(s + 1) * L]
        if cfg.sequential:
            hist_out_ref[0] = z_ref[last_sub, L + 1:, :]
        else:
            for s in range(cfg.n_sub):
                hist_out_ref[s] = z_ref[s, L + 1:, :]

    @piece("gb_proj")
    def _():
        val["gb"] = proj(3 * D_A + D_B, D_IN)

    for h in range(N_HEADS_A):
        @piece(f"ln_v{h}")
        def _(h=h):
            sl = slice(h * HEAD_A, (h + 1) * HEAD_A)
            vn = _layernorm(val["v"][:, sl], lnv_g_ref[:, sl], lnv_b_ref[:, sl])
            if v_out_ref is not None:
                v_out_ref[0, :, sl] = vn
            vnb = vn.astype(jnp.bfloat16)
            for s in range(cfg.n_sub):
                for c0 in range(0, L, MLP_CHUNK):
                    lc = min(MLP_CHUNK, L - c0)
                    r0 = s * L + c0
                    mixed = _dot(ws_ref[h, :lc, :lc], vnb[r0:r0 + lc])
                    mix_ref[r0:r0 + lc, sl] = (mixed + bs_ref[:lc, sl]).astype(jnp.bfloat16)

    for n in range(N_COL_PIECES):
        @piece(f"a_out{n}")
        def _(n=n):
            cols = col_piece(n)
            k, off = divmod(n * COL_PIECE, ga_width)
            ga = val[f"ga{k}"][:, off:off + COL_PIECE]
            a_out = val["u"][:, cols] * mix_ref[:, cols].astype(jnp.float32) * jax.nn.silu(ga)
            cat_ref[:, cols] = a_out.astype(jnp.bfloat16)

    for g, w in enumerate(POOL_WINDOWS):
        @piece(f"pooling{g}")
        def _(g=g, w=w):
            cols, ccols = col_piece(g), col_piece(g, D_A)
            for s in range(cfg.n_sub):
                rows = slice(s * L, (s + 1) * L)
                acc = z_ref[s, :, cols]
                span = 1
                while span < w:
                    acc = acc + pltpu.roll(acc, span, axis=0)
                    span *= 2
                win = acc[HIST_ROWS:]
                hp_g = val["hp"][rows, cols]
                mix_ref[rows, ccols] = (win * (1.0 / w) - hp_g).astype(jnp.bfloat16)
                if cfg.sequential and s == 0:
                    cnt = jnp.where(j == 0, cnt_ref[:, cols], float(w))
                    head = win[:HIST_ROWS] / cnt - hp_g[:HIST_ROWS]
                    mix_ref[:HIST_ROWS, ccols] = head.astype(jnp.bfloat16)

        @piece(f"b_out{g}")
        def _(g=g):
            cols, ccols = col_piece(g), col_piece(g, D_A)
            q = _dot(mix_ref[:, ccols], pool_w_ref[g])
            b_outp = q * pool_scale_ref[:, cols] * jax.nn.silu(val["gb"][:, cols])
            cat_ref[:, ccols] = b_outp.astype(jnp.bfloat16)

    @piece("w_out")
    def _():
        y = _dot(cat_ref[...], w_out_ref[...]) + b_out_ref[...]
        val["pre"] = ALPHA * xres_ref[0] + y

    @piece("post_norm")
    def _():
        xn = val["xn"] = _layernorm(val["pre"], ln_g_ref[...], ln_b_ref[...])
        val["xnb"] = xn.astype(jnp.bfloat16)
        val["pb"] = p_ref[0].astype(jnp.bfloat16)

    gate_width = D_MODEL // GATE_DOTS
    for n in range(GATE_DOTS):
        @piece(f"gate{n}")
        def _(n=n):
            cols = slice(n * gate_width, (n + 1) * gate_width)
            gate = jax.nn.sigmoid(_dot(val["xnb"], w_pg_ref[:, cols]) + b_pg_ref[:, cols])
            ple = _dot(val["pb"], w_ple_ref[:, cols])
            y_ref[0, :, cols] = val["xn"][:, cols] + gate * ple

    assert set(pieces) == set(STAGE1_PIECES + STAGE2_PIECES)
    for name in (PIPELINE_ORDER if cfg.pipelined else STAGE1_PIECES + STAGE2_PIECES):
        pieces[name]()


def _resident(shape):
    nd = len(shape)
    return pl.BlockSpec(shape, lambda i: (0,) * nd, pipeline_mode=pl.Buffered(1))


def _run_layer(cfg, x, p, hist, cnt, weights):
    T = cfg.tile_rows
    n_seq = cfg.n_tiles // cfg.n_inner
    n_hist = 1 if cfg.sequential else cfg.n_sub
    assert x.shape == (n_seq, cfg.n_inner * T, D_MODEL) and p.shape == (n_seq, cfg.n_inner * T, D_PLE)
    assert hist.shape == ((1 if cfg.sequential else n_seq * n_hist), POOL_HIST, D_B)
    assert cfg.sequential or cfg.n_inner == 1

    def cur(i):
        t = jnp.minimum(i, cfg.n_tiles - 1)
        return t // cfg.n_inner, t % cfg.n_inner

    def res(i):
        t = jnp.maximum(i - 1, 0) if cfg.pipelined else i
        return t // cfg.n_inner, t % cfg.n_inner

    in_specs = [
        pl.BlockSpec((1, T, D_MODEL), lambda i: (*cur(i), 0)),
        pl.BlockSpec((1, T, D_MODEL), lambda i: (*res(i), 0)),
        pl.BlockSpec((1, T, D_PLE), lambda i: (*res(i), 0)),
        pl.BlockSpec((n_hist, POOL_HIST, D_B), lambda i: ((0 if cfg.sequential else cur(i)[0]), 0, 0)),
        _resident(cnt.shape),
    ] + [_resident(w.shape) for w in weights]
    out_shape = [jax.ShapeDtypeStruct(x.shape, jnp.float32),
                 jax.ShapeDtypeStruct((n_seq * n_hist, POOL_HIST, D_B), jnp.float32)]
    out_specs = [pl.BlockSpec((1, T, D_MODEL), lambda i: (*res(i), 0)),
                 pl.BlockSpec((n_hist, POOL_HIST, D_B), lambda i: (cur(i)[0], 0, 0))]
    if cfg.emit_v:
        out_shape.append(jax.ShapeDtypeStruct(x.shape[:2] + (D_A,), jnp.float32))
        out_specs.append(pl.BlockSpec((1, T, D_A), lambda i: (*cur(i), 0)))
    return pl.pallas_call(
        functools.partial(_layer_kernel, cfg),
        grid=(cfg.n_tiles + (1 if cfg.pipelined else 0),),
        in_specs=in_specs,
        out_specs=out_specs,
        out_shape=out_shape,
        scratch_shapes=[pltpu.VMEM((cfg.n_sub, HIST_ROWS + cfg.sub_len, D_B), jnp.float32),
                        pltpu.VMEM((T, D_MIX), jnp.bfloat16),
                        pltpu.VMEM((T, D_MIX), jnp.bfloat16)],
        compiler_params=pltpu.CompilerParams(dimension_semantics=("arbitrary",),
                                             vmem_limit_bytes=VMEM_LIMIT_BYTES),
        name="layer_sample" if cfg.emit_v else "layer_prompt",
    )(x, x, p, hist, cnt, *weights)


def kernel(x_prompt, x_sample, state_pool, p_prompt, p_sample, w_in, ln_v_g, ln_v_b, w_s, b_s, pool_w, pool_scale, w_out, b_out, ln_g, ln_b, w_ple, w_pg, b_pg):
    assert DEPTH == 1 and w_in.shape[0] == DEPTH
    B, S, _ = x_prompt.shape
    Bs, Ss, _ = x_sample.shape
    bf = jnp.bfloat16
    row = lambda a: a.reshape(1, -1).astype(jnp.float32)

    blk = np.arange(MLP_CHUNK) // CHUNK
    mask = jnp.asarray(blk[None, :] <= blk[:, None])
    ws_masked = jnp.where(mask[None], w_s[0], jnp.zeros((), w_s.dtype)).astype(bf)
    bs_tbl = jnp.repeat(jnp.transpose(b_s[0]), HEAD_A, axis=1).astype(jnp.float32)
    win = np.repeat(np.asarray(POOL_WINDOWS, np.float32), POOL_GROUP)[None, :]
    cnt = jnp.asarray(np.minimum(np.arange(1, HIST_ROWS + 1, dtype=np.float32)[:, None], win))
    weights = (w_in[0].astype(bf), row(ln_v_g[0]), row(ln_v_b[0]), ws_masked, bs_tbl, pool_w[0].astype(bf),
               row(pool_scale[0]), w_out[0].astype(bf), row(b_out[0]), row(ln_g[0]), row(ln_b[0]),
               w_ple[0].astype(bf), w_pg[0].astype(bf), row(b_pg[0]))

    n_inner = S // PROMPT_TILE_ROWS
    cfg_p = TileConfig(n_sub=1, sub_len=PROMPT_TILE_ROWS, n_inner=n_inner, n_tiles=B * n_inner,
                       sequential=True, pipelined=True, emit_v=False)
    hist_p = jnp.zeros((1, POOL_HIST, D_B), jnp.float32)
    y_p, hist_p_new = _run_layer(cfg_p, x_prompt, p_prompt[0], hist_p, cnt, weights)

    assert PAST_LEN >= POOL_HIST
    ns = SAMPLE_STREAMS_PER_TILE
    cfg_s = TileConfig(n_sub=ns, sub_len=Ss, n_inner=1, n_tiles=Bs // ns,
                       sequential=False, pipelined=True, emit_v=True)
    hist_s = state_pool[0]
    y_s, hist_s_new, v_s = _run_layer(
        cfg_s, x_sample.reshape(Bs // ns, ns * Ss, D_MODEL),
        p_sample[0].reshape(Bs // ns, ns * Ss, D_PLE), hist_s, cnt, weights)

    return (y_p,
            y_s.reshape(Bs, Ss, D_MODEL),
            hist_p_new[None],
            hist_s_new[None],
            v_s.reshape(1, Bs, Ss, N_HEADS_A, HEAD_A))
```

```python
import functools
from typing import NamedTuple

import numpy as np
import jax
import jax.numpy as jnp
from jax.experimental import pallas as pl
from jax.experimental.pallas import tpu as pltpu

D_MODEL = 1024
DEPTH = 1
PAST_LEN = 4096
CHUNK = 64
MLP_CHUNK = 128
D_MIX = 2 * D_MODEL
D_A = D_MIX // 2
HEAD_A = 128
N_HEADS_A = D_A // HEAD_A
D_B = D_MIX - D_A
POOL_WINDOWS = (2, 4, 8, 16)
N_POOL_GROUPS = len(POOL_WINDOWS)
POOL_GROUP = D_B // N_POOL_GROUPS
POOL_HIST = max(POOL_WINDOWS) - 1
HIST_ROWS = POOL_HIST + 1
HIST_OUT_SEQS = 8
D_PLE = 256
D_IN = 3 * D_A + 2 * D_B
ALPHA = (2.0 * DEPTH) ** 0.25
LN_EPS = 1e-5

PROMPT_TILE_ROWS = 512
SAMPLE_STREAMS_PER_TILE = 8
VMEM_LIMIT_BYTES = 56 * 1024 * 1024
COL_PIECE = 256
N_COL_PIECES = D_MODEL // COL_PIECE
assert COL_PIECE == POOL_GROUP and COL_PIECE == 2 * HEAD_A


def _numbered(name, n):
    return tuple(f"{name}{i}" for i in range(n))


GA_DOTS = N_COL_PIECES
GATE_DOTS = N_COL_PIECES
STAGE1_PIECES = (("v_proj", "u_proj") + _numbered("ga_proj", GA_DOTS) + _numbered("ln_v", N_HEADS_A)
                 + _numbered("a_out", N_COL_PIECES) + ("hp_proj", "gb_proj") + _numbered("pooling", N_POOL_GROUPS)
                 + _numbered("b_out", N_POOL_GROUPS))
STAGE2_PIECES = ("w_out", "post_norm") + _numbered("gate", GATE_DOTS)
PIPELINE_ORDER = (
    "v_proj", "w_out", "u_proj",
    "ln_v0", "ln_v1", "ga_proj0", "ln_v2", "ln_v3", "ga_proj1",
    "ln_v4", "ln_v5", "ga_proj2", "ln_v6", "ln_v7", "ga_proj3",
    "post_norm", "a_out0", "a_out1", "a_out2", "a_out3", "hp_proj",
    "pooling0", "pooling1", "pooling2", "pooling3",
    "gate0", "gate1", "gate2", "gate3", "gb_proj",
    "b_out0", "b_out1", "b_out2", "b_out3")
assert sorted(PIPELINE_ORDER) == sorted(STAGE1_PIECES + STAGE2_PIECES)


class TileConfig(NamedTuple):
    n_sub: int
    sub_len: int
    n_inner: int
    n_tiles: int
    sequential: bool
    pipelined: bool
    emit_v: bool

    @property
    def tile_rows(self):
        return self.n_sub * self.sub_len


def _layernorm(x, g, b):
    mu = jnp.mean(x, axis=-1, keepdims=True)
    xc = x - mu
    var = jnp.mean(xc * xc, axis=-1, keepdims=True)
    return xc * jax.lax.rsqrt(var + LN_EPS) * g + b


def _dot(a, b):
    return jnp.dot(a, b, preferred_element_type=jnp.float32)


def _history_to_rows(z_ref, s, row0, hist_ref, col):
    for r in range(POOL_HIST):
        z_ref[s, row0 + r:row0 + r + 1, :] = hist_ref[r, col:col + 1, :]


def _rows_to_history(hist_out_ref, col, z_ref, s, row0):
    for r in range(POOL_HIST):
        hist_out_ref[r, col:col + 1, :] = z_ref[s, row0 + r:row0 + r + 1, :]


def _layer_kernel(cfg, x_ref, xres_ref, p_ref, hist_ref, cnt_ref, w_in_ref, lnv_g_ref, lnv_b_ref, ws_ref, bs_ref,
                  pool_w_ref, pool_scale_ref, w_out_ref, b_out_ref, ln_g_ref, ln_b_ref, w_ple_ref,
                  w_pg_ref, b_pg_ref, *out_and_scratch):
    if cfg.emit_v:
        y_ref, hist_out_ref, v_out_ref, z_ref, mix_ref, cat_ref = out_and_scratch
    else:
        y_ref, hist_out_ref, z_ref, mix_ref, cat_ref = out_and_scratch
        v_out_ref = None
    L = cfg.sub_len
    last_sub = cfg.n_sub - 1
    step = pl.program_id(0)
    j = jax.lax.rem(jnp.minimum(step, cfg.n_tiles - 1), cfg.n_inner)

    if cfg.sequential:
        @pl.when(j == 0)
        def _():
            z_ref[last_sub, L:L + 1, :] = jnp.zeros((1, D_B), jnp.float32)
            _history_to_rows(z_ref, last_sub, L + 1, hist_ref, 0)

    if cfg.pipelined:
        @pl.when(step == 0)
        def _():
            cat_ref[...] = jnp.zeros_like(cat_ref)

    val = {}
    pieces = {}

    def piece(name):
        def register(fn):
            pieces[name] = fn
        return register

    def col_piece(n, base=0):
        return slice(base + n * COL_PIECE, base + (n + 1) * COL_PIECE)

    def proj(lo, hi):
        return _dot(val["xb"], w_in_ref[:, lo:hi])

    @piece("v_proj")
    def _():
        val["xb"] = x_ref[0].astype(jnp.bfloat16)
        val["v"] = proj(D_A, 2 * D_A)

    @piece("u_proj")
    def _():
        val["u"] = proj(0, D_A)

    ga_width = D_A // GA_DOTS
    for n in range(GA_DOTS):
        @piece(f"ga_proj{n}")
        def _(n=n):
            val[f"ga{n}"] = proj(2 * D_A + n * ga_width, 2 * D_A + (n + 1) * ga_width)

    @piece("hp_proj")
    def _():
        hp = val["hp"] = proj(3 * D_A, 3 * D_A + D_B)
        if cfg.sequential:
            z_ref[0, :HIST_ROWS, :] = z_ref[last_sub, L:, :]
        else:
            z_ref[:, :1, :] = jnp.zeros((cfg.n_sub, 1, D_B), jnp.float32)
            for s in range(cfg.n_sub):
                _history_to_rows(z_ref, s, 1, hist_ref, s)
        for s in range(cfg.n_sub):
            if cfg.sequential and s > 0:
                z_ref[s, :HIST_ROWS, :] = z_ref[s - 1, L:, :]
            z_ref[s, HIST_ROWS:, :] = hp[s * L:(s + 1) * L]
        if not cfg.sequential:
            for s in range(cfg.n_sub):
                _rows_to_history(hist_out_ref, s, z_ref, s, L + 1)

    @piece("gb_proj")
    def _():
        val["gb"] = proj(3 * D_A + D_B, D_IN)

    for h in range(N_HEADS_A):
        @piece(f"ln_v{h}")
        def _(h=h):
            sl = slice(h * HEAD_A, (h + 1) * HEAD_A)
            vn = _layernorm(val["v"][:, sl], lnv_g_ref[:, sl], lnv_b_ref[:, sl])
            if v_out_ref is not None:
                v_out_ref[0, :, sl] = vn
            vnb = vn.astype(jnp.bfloat16)
            for s in range(cfg.n_sub):
                for c0 in range(0, L, MLP_CHUNK):
                    lc = min(MLP_CHUNK, L - c0)
                    r0 = s * L + c0
                    mixed = _dot(ws_ref[h, :lc, :lc], vnb[r0:r0 + lc])
                    mix_ref[r0:r0 + lc, sl] = (mixed + bs_ref[:lc, sl]).astype(jnp.bfloat16)

    for n in range(N_COL_PIECES):
        @piece(f"a_out{n}")
        def _(n=n):
            cols = col_piece(n)
            k, off = divmod(n * COL_PIECE, ga_width)
            ga = val[f"ga{k}"][:, off:off + COL_PIECE]
            a_out = val["u"][:, cols] * mix_ref[:, cols].astype(jnp.float32) * jax.nn.silu(ga)
            cat_ref[:, cols] = a_out.astype(jnp.bfloat16)

    for g, w in enumerate(POOL_WINDOWS):
        @piece(f"pooling{g}")
        def _(g=g, w=w):
            cols, ccols = col_piece(g), col_piece(g, D_A)
            for s in range(cfg.n_sub):
                rows = slice(s * L, (s + 1) * L)
                acc = z_ref[s, :, cols]
                span = 1
                while span < w:
                    acc = acc + pltpu.roll(acc, span, axis=0)
                    span *= 2
                win = acc[HIST_ROWS:]
                hp_g = val["hp"][rows, cols]
                mix_ref[rows, ccols] = (win * (1.0 / w) - hp_g).astype(jnp.bfloat16)
                if cfg.sequential and s == 0:
                    cnt = jnp.where(j == 0, cnt_ref[:, cols], float(w))
                    head = win[:HIST_ROWS] / cnt - hp_g[:HIST_ROWS]
                    mix_ref[:HIST_ROWS, ccols] = head.astype(jnp.bfloat16)

        @piece(f"b_out{g}")
        def _(g=g):
            cols, ccols = col_piece(g), col_piece(g, D_A)
            q = _dot(mix_ref[:, ccols], pool_w_ref[g])
            b_outp = q * pool_scale_ref[:, cols] * jax.nn.silu(val["gb"][:, cols])
            cat_ref[:, ccols] = b_outp.astype(jnp.bfloat16)

    @piece("w_out")
    def _():
        y = _dot(cat_ref[...], w_out_ref[...]) + b_out_ref[...]
        val["pre"] = ALPHA * xres_ref[0] + y

    @piece("post_norm")
    def _():
        xn = val["xn"] = _layernorm(val["pre"], ln_g_ref[...], ln_b_ref[...])
        val["xnb"] = xn.astype(jnp.bfloat16)
        val["pb"] = p_ref[0].astype(jnp.bfloat16)

    gate_width = D_MODEL // GATE_DOTS
    for n in range(GATE_DOTS):
        @piece(f"gate{n}")
        def _(n=n):
            cols = slice(n * gate_width, (n + 1) * gate_width)
            gate = jax.nn.sigmoid(_dot(val["xnb"], w_pg_ref[:, cols]) + b_pg_ref[:, cols])
            ple = _dot(val["pb"], w_ple_ref[:, cols])
            y_ref[0, :, cols] = val["xn"][:, cols] + gate * ple

    assert set(pieces) == set(STAGE1_PIECES + STAGE2_PIECES)
    for name in (PIPELINE_ORDER if cfg.pipelined else STAGE1_PIECES + STAGE2_PIECES):
        pieces[name]()

    if cfg.sequential:
        @pl.when(j == cfg.n_inner - 1)
        def _():
            seq = jnp.minimum(step, cfg.n_tiles - 1) // cfg.n_inner
            for col in range(HIST_OUT_SEQS):
                @pl.when(jax.lax.rem(seq, HIST_OUT_SEQS) == col)
                def _(col=col):
                    _rows_to_history(hist_out_ref, col, z_ref, last_sub, L + 1)


def _resident(shape):
    nd = len(shape)
    return pl.BlockSpec(shape, lambda i: (0,) * nd, pipeline_mode=pl.Buffered(1))


def _run_layer(cfg, x, p, hist, cnt, weights):
    T = cfg.tile_rows
    n_seq = cfg.n_tiles // cfg.n_inner
    n_hist_cols = n_seq if cfg.sequential else cfg.n_tiles * cfg.n_sub
    hist_blk = HIST_OUT_SEQS if cfg.sequential else cfg.n_sub
    assert x.shape == (n_seq, cfg.n_inner * T, D_MODEL) and p.shape == (n_seq, cfg.n_inner * T, D_PLE)
    assert hist.shape == (POOL_HIST, (HIST_OUT_SEQS if cfg.sequential else n_hist_cols), D_B)
    assert n_hist_cols % hist_blk == 0 and hist_blk % 8 == 0
    assert cfg.sequential or cfg.n_inner == 1

    def cur(i):
        t = jnp.minimum(i, cfg.n_tiles - 1)
        return t // cfg.n_inner, t % cfg.n_inner

    def res(i):
        t = jnp.maximum(i - 1, 0) if cfg.pipelined else i
        return t // cfg.n_inner, t % cfg.n_inner

    in_specs = [
        pl.BlockSpec((1, T, D_MODEL), lambda i: (*cur(i), 0)),
        pl.BlockSpec((1, T, D_MODEL), lambda i: (*res(i), 0)),
        pl.BlockSpec((1, T, D_PLE), lambda i: (*res(i), 0)),
        pl.BlockSpec((POOL_HIST, hist_blk, D_B), lambda i: (0, (0 if cfg.sequential else cur(i)[0]), 0)),
        _resident(cnt.shape),
    ] + [_resident(w.shape) for w in weights]
    out_shape = [jax.ShapeDtypeStruct(x.shape, jnp.float32),
                 jax.ShapeDtypeStruct((POOL_HIST, n_hist_cols, D_B), jnp.float32)]
    out_specs = [pl.BlockSpec((1, T, D_MODEL), lambda i: (*res(i), 0)),
                 pl.BlockSpec((POOL_HIST, hist_blk, D_B),
                              lambda i: (0, (cur(i)[0] // HIST_OUT_SEQS if cfg.sequential else cur(i)[0]), 0))]
    if cfg.emit_v:
        out_shape.append(jax.ShapeDtypeStruct(x.shape[:2] + (D_A,), jnp.float32))
        out_specs.append(pl.BlockSpec((1, T, D_A), lambda i: (*cur(i), 0)))
    return pl.pallas_call(
        functools.partial(_layer_kernel, cfg),
        grid=(cfg.n_tiles + (1 if cfg.pipelined else 0),),
        in_specs=in_specs,
        out_specs=out_specs,
        out_shape=out_shape,
        scratch_shapes=[pltpu.VMEM((cfg.n_sub, HIST_ROWS + cfg.sub_len, D_B), jnp.float32),
                        pltpu.VMEM((T, D_MIX), jnp.bfloat16),
                        pltpu.VMEM((T, D_MIX), jnp.bfloat16)],
        compiler_params=pltpu.CompilerParams(dimension_semantics=("arbitrary",),
                                             vmem_limit_bytes=VMEM_LIMIT_BYTES),
        name="layer_sample" if cfg.emit_v else "layer_prompt",
    )(x, x, p, hist, cnt, *weights)


def kernel(x_prompt, x_sample, state_pool, p_prompt, p_sample, w_in, ln_v_g, ln_v_b, w_s, b_s, pool_w, pool_scale, w_out, b_out, ln_g, ln_b, w_ple, w_pg, b_pg):
    assert DEPTH == 1 and w_in.shape[0] == DEPTH
    B, S, _ = x_prompt.shape
    Bs, Ss, _ = x_sample.shape
    bf = jnp.bfloat16
    row = lambda a: a.reshape(1, -1).astype(jnp.float32)

    blk = np.arange(MLP_CHUNK) // CHUNK
    mask = jnp.asarray(blk[None, :] <= blk[:, None])
    ws_masked = jnp.where(mask[None], w_s[0], jnp.zeros((), w_s.dtype)).astype(bf)
    bs_tbl = jnp.repeat(jnp.transpose(b_s[0]), HEAD_A, axis=1).astype(jnp.float32)
    win = np.repeat(np.asarray(POOL_WINDOWS, np.float32), POOL_GROUP)[None, :]
    cnt = jnp.asarray(np.minimum(np.arange(1, HIST_ROWS + 1, dtype=np.float32)[:, None], win))
    weights = (w_in[0].astype(bf), row(ln_v_g[0]), row(ln_v_b[0]), ws_masked, bs_tbl, pool_w[0].astype(bf),
               row(pool_scale[0]), w_out[0].astype(bf), row(b_out[0]), row(ln_g[0]), row(ln_b[0]),
               w_ple[0].astype(bf), w_pg[0].astype(bf), row(b_pg[0]))

    n_inner = S // PROMPT_TILE_ROWS
    cfg_p = TileConfig(n_sub=1, sub_len=PROMPT_TILE_ROWS, n_inner=n_inner, n_tiles=B * n_inner,
                       sequential=True, pipelined=True, emit_v=False)
    hist_p = jnp.zeros((POOL_HIST, HIST_OUT_SEQS, D_B), jnp.float32)
    y_p, hist_p_new = _run_layer(cfg_p, x_prompt, p_prompt[0], hist_p, cnt, weights)

    assert PAST_LEN >= POOL_HIST
    ns = SAMPLE_STREAMS_PER_TILE
    cfg_s = TileConfig(n_sub=ns, sub_len=Ss, n_inner=1, n_tiles=Bs // ns,
                       sequential=False, pipelined=False, emit_v=True)
    hist_s = jnp.transpose(state_pool[0], (1, 0, 2))
    y_s, hist_s_new, v_s = _run_layer(
        cfg_s, x_sample.reshape(Bs // ns, ns * Ss, D_MODEL),
        p_sample[0].reshape(Bs // ns, ns * Ss, D_PLE), hist_s, cnt, weights)

    return (y_p,
            y_s.reshape(Bs, Ss, D_MODEL),
            jnp.transpose(hist_p_new, (1, 0, 2))[None],
            jnp.transpose(hist_s_new, (1, 0, 2))[None],
            v_s.reshape(1, Bs, Ss, N_HEADS_A, HEAD_A))
```

```python
import functools
from typing import NamedTuple

import numpy as np
import jax
import jax.numpy as jnp
from jax.experimental import pallas as pl
from jax.experimental.pallas import tpu as pltpu

D_MODEL = 1024
DEPTH = 1
PAST_LEN = 4096
CHUNK = 64
MLP_CHUNK = 128
D_MIX = 2 * D_MODEL
D_A = D_MIX // 2
HEAD_A = 128
N_HEADS_A = D_A // HEAD_A
D_B = D_MIX - D_A
POOL_WINDOWS = (2, 4, 8, 16)
N_POOL_GROUPS = len(POOL_WINDOWS)
POOL_GROUP = D_B // N_POOL_GROUPS
POOL_HIST = max(POOL_WINDOWS) - 1
HIST_ROWS = POOL_HIST + 1
HIST_OUT_SEQS = 8
N_VECS, VEC_ROWS = 7, 8
D_PLE = 256
D_IN = 3 * D_A + 2 * D_B
ALPHA = (2.0 * DEPTH) ** 0.25
LN_EPS = 1e-5

PROMPT_TILE_ROWS = 512
SAMPLE_STREAMS_PER_TILE = 8
VMEM_LIMIT_BYTES = 56 * 1024 * 1024
COL_PIECE = 256
N_COL_PIECES = D_MODEL // COL_PIECE
assert COL_PIECE == POOL_GROUP and COL_PIECE == 2 * HEAD_A


def _numbered(name, n):
    return tuple(f"{name}{i}" for i in range(n))


GA_DOTS = N_COL_PIECES
GATE_DOTS = N_COL_PIECES
STAGE1_PIECES = (("v_proj", "u_proj") + _numbered("ga_proj", GA_DOTS) + _numbered("ln_v", N_HEADS_A)
                 + _numbered("a_out", N_COL_PIECES) + ("hp_proj", "gb_proj") + _numbered("pooling", N_POOL_GROUPS)
                 + _numbered("b_out", N_POOL_GROUPS))
STAGE2_PIECES = ("w_out", "post_norm") + _numbered("gate", GATE_DOTS)
PIPELINE_ORDER = (
    "v_proj", "w_out", "u_proj",
    "ln_v0", "ln_v1", "ga_proj0", "ln_v2", "ln_v3", "ga_proj1",
    "ln_v4", "ln_v5", "ga_proj2", "ln_v6", "ln_v7", "ga_proj3",
    "post_norm", "a_out0", "a_out1", "a_out2", "a_out3", "hp_proj",
    "pooling0", "pooling1", "pooling2", "pooling3",
    "gate0", "gate1", "gate2", "gate3", "gb_proj",
    "b_out0", "b_out1", "b_out2", "b_out3")
assert sorted(PIPELINE_ORDER) == sorted(STAGE1_PIECES + STAGE2_PIECES)


class TileConfig(NamedTuple):
    n_sub: int
    sub_len: int
    n_inner: int
    n_tiles: int
    sequential: bool
    pipelined: bool
    emit_v: bool

    @property
    def tile_rows(self):
        return self.n_sub * self.sub_len


def _layernorm(x, g, b):
    mu = jnp.mean(x, axis=-1, keepdims=True)
    xc = x - mu
    var = jnp.mean(xc * xc, axis=-1, keepdims=True)
    return xc * jax.lax.rsqrt(var + LN_EPS) * g + b


def _dot(a, b):
    return jnp.dot(a, b, preferred_element_type=jnp.float32)


def _history_to_rows(z_ref, s, row0, hist_ref, col):
    for r in range(POOL_HIST):
        z_ref[s, row0 + r:row0 + r + 1, :] = hist_ref[r, col:col + 1, :]


def _rows_to_history(hist_out_ref, col, z_ref, s, row0):
    for r in range(POOL_HIST):
        hist_out_ref[r, col:col + 1, :] = z_ref[s, row0 + r:row0 + r + 1, :]


def _layer_kernel(cfg, x_ref, xres_ref, p_ref, hist_ref, cnt_ref, vecs_ref, w_in_ref, ws_ref, bs_ref,
                  pool_w_ref, w_out_ref, w_ple_ref, w_pg_ref, *out_and_scratch):
    lnv_g_ref, lnv_b_ref, pool_scale_ref, b_out_ref, ln_g_ref, ln_b_ref, b_pg_ref = (
        vecs_ref.at[k:k + 1] for k in range(N_VECS))
    if cfg.emit_v:
        y_ref, hist_out_ref, v_out_ref, z_ref, mix_ref, cat_ref = out_and_scratch
    else:
        y_ref, hist_out_ref, z_ref, mix_ref, cat_ref = out_and_scratch
        v_out_ref = None
    L = cfg.sub_len
    last_sub = cfg.n_sub - 1
    step = pl.program_id(0)
    j = jax.lax.rem(jnp.minimum(step, cfg.n_tiles - 1), cfg.n_inner)

    if cfg.sequential:
        @pl.when(j == 0)
        def _():
            z_ref[last_sub, L:L + 1, :] = jnp.zeros((1, D_B), jnp.float32)
            _history_to_rows(z_ref, last_sub, L + 1, hist_ref, 0)

    if cfg.pipelined:
        @pl.when(step == 0)
        def _():
            cat_ref[...] = jnp.zeros_like(cat_ref)

    val = {}
    pieces = {}

    def piece(name):
        def register(fn):
            pieces[name] = fn
        return register

    def col_piece(n, base=0):
        return slice(base + n * COL_PIECE, base + (n + 1) * COL_PIECE)

    def proj(lo, hi):
        return _dot(val["xb"], w_in_ref[:, lo:hi])

    @piece("v_proj")
    def _():
        val["xb"] = x_ref[0].astype(jnp.bfloat16)
        val["v"] = proj(D_A, 2 * D_A)

    @piece("u_proj")
    def _():
        val["u"] = proj(0, D_A)

    ga_width = D_A // GA_DOTS
    for n in range(GA_DOTS):
        @piece(f"ga_proj{n}")
        def _(n=n):
            val[f"ga{n}"] = proj(2 * D_A + n * ga_width, 2 * D_A + (n + 1) * ga_width)

    @piece("hp_proj")
    def _():
        hp = val["hp"] = proj(3 * D_A, 3 * D_A + D_B)
        if cfg.sequential:
            z_ref[0, :HIST_ROWS, :] = z_ref[last_sub, L:, :]
        else:
            z_ref[:, :1, :] = jnp.zeros((cfg.n_sub, 1, D_B), jnp.float32)
            for s in range(cfg.n_sub):
                _history_to_rows(z_ref, s, 1, hist_ref, s)
        for s in range(cfg.n_sub):
            if cfg.sequential and s > 0:
                z_ref[s, :HIST_ROWS, :] = z_ref[s - 1, L:, :]
            z_ref[s, HIST_ROWS:, :] = hp[s * L:(s + 1) * L]
        if not cfg.sequential:
            for s in range(cfg.n_sub):
                _rows_to_history(hist_out_ref, s, z_ref, s, L + 1)

    @piece("gb_proj")
    def _():
        val["gb"] = proj(3 * D_A + D_B, D_IN)

    for h in range(N_HEADS_A):
        @piece(f"ln_v{h}")
        def _(h=h):
            sl = slice(h * HEAD_A, (h + 1) * HEAD_A)
            vn = _layernorm(val["v"][:, sl], lnv_g_ref[:, sl], lnv_b_ref[:, sl])
            if v_out_ref is not None:
                v_out_ref[0, :, sl] = vn
            vnb = vn.astype(jnp.bfloat16)
            for s in range(cfg.n_sub):
                for c0 in range(0, L, MLP_CHUNK):
                    lc = min(MLP_CHUNK, L - c0)
                    r0 = s * L + c0
                    mixed = _dot(ws_ref[h, :lc, :lc], vnb[r0:r0 + lc])
                    mix_ref[r0:r0 + lc, sl] = (mixed + bs_ref[:lc, sl]).astype(jnp.bfloat16)

    for n in range(N_COL_PIECES):
        @piece(f"a_out{n}")
        def _(n=n):
            cols = col_piece(n)
            k, off = divmod(n * COL_PIECE, ga_width)
            ga = val[f"ga{k}"][:, off:off + COL_PIECE]
            a_out = val["u"][:, cols] * mix_ref[:, cols].astype(jnp.float32) * jax.nn.silu(ga)
            cat_ref[:, cols] = a_out.astype(jnp.bfloat16)

    for g, w in enumerate(POOL_WINDOWS):
        @piece(f"pooling{g}")
        def _(g=g, w=w):
            cols, ccols = col_piece(g), col_piece(g, D_A)
            for s in range(cfg.n_sub):
                rows = slice(s * L, (s + 1) * L)
                acc = z_ref[s, :, cols]
                span = 1
                while span < w:
                    acc = acc + pltpu.roll(acc, span, axis=0)
                    span *= 2
                win = acc[HIST_ROWS:]
                hp_g = val["hp"][rows, cols]
                mix_ref[rows, ccols] = (win * (1.0 / w) - hp_g).astype(jnp.bfloat16)
                if cfg.sequential and s == 0:
                    cnt = jnp.where(j == 0, cnt_ref[:, cols], float(w))
                    head = win[:HIST_ROWS] / cnt - hp_g[:HIST_ROWS]
                    mix_ref[:HIST_ROWS, ccols] = head.astype(jnp.bfloat16)

        @piece(f"b_out{g}")
        def _(g=g):
            cols, ccols = col_piece(g), col_piece(g, D_A)
            q = _dot(mix_ref[:, ccols], pool_w_ref[g])
            b_outp = q * pool_scale_ref[:, cols] * jax.nn.silu(val["gb"][:, cols])
            cat_ref[:, ccols] = b_outp.astype(jnp.bfloat16)

    @piece("w_out")
    def _():
        y = _dot(cat_ref[...], w_out_ref[...]) + b_out_ref[...]
        val["pre"] = ALPHA * xres_ref[0] + y

    @piece("post_norm")
    def _():
        xn = val["xn"] = _layernorm(val["pre"], ln_g_ref[...], ln_b_ref[...])
        val["xnb"] = xn.astype(jnp.bfloat16)
        val["pb"] = p_ref[0].astype(jnp.bfloat16)

    gate_width = D_MODEL // GATE_DOTS
    for n in range(GATE_DOTS):
        @piece(f"gate{n}")
        def _(n=n):
            cols = slice(n * gate_width, (n + 1) * gate_width)
            gate = jax.nn.sigmoid(_dot(val["xnb"], w_pg_ref[:, cols]) + b_pg_ref[:, cols])
            ple = _dot(val["pb"], w_ple_ref[:, cols])
            y_ref[0, :, cols] = val["xn"][:, cols] + gate * ple

    assert set(pieces) == set(STAGE1_PIECES + STAGE2_PIECES)
    for name in (PIPELINE_ORDER if cfg.pipelined else STAGE1_PIECES + STAGE2_PIECES):
        pieces[name]()

    if cfg.sequential:
        @pl.when(j == cfg.n_inner - 1)
        def _():
            seq = jnp.minimum(step, cfg.n_tiles - 1) // cfg.n_inner
            for col in range(HIST_OUT_SEQS):
                @pl.when(jax.lax.rem(seq, HIST_OUT_SEQS) == col)
                def _(col=col):
                    _rows_to_history(hist_out_ref, col, z_ref, last_sub, L + 1)


def _resident(shape):
    nd = len(shape)
    return pl.BlockSpec(shape, lambda i: (0,) * nd, pipeline_mode=pl.Buffered(1))


def _run_layer(cfg, x, p, hist, cnt, weights):
    T = cfg.tile_rows
    n_seq = cfg.n_tiles // cfg.n_inner
    n_hist_cols = n_seq if cfg.sequential else cfg.n_tiles * cfg.n_sub
    hist_blk = HIST_OUT_SEQS if cfg.sequential else cfg.n_sub
    assert x.shape == (n_seq, cfg.n_inner * T, D_MODEL) and p.shape == (n_seq, cfg.n_inner * T, D_PLE)
    assert hist.shape == (POOL_HIST, (HIST_OUT_SEQS if cfg.sequential else n_hist_cols), D_B)
    assert n_hist_cols % hist_blk == 0 and hist_blk % 8 == 0
    assert cfg.sequential or cfg.n_inner == 1

    def cur(i):
        t = jnp.minimum(i, cfg.n_tiles - 1)
        return t // cfg.n_inner, t % cfg.n_inner

    def res(i):
        t = jnp.maximum(i - 1, 0) if cfg.pipelined else i
        return t // cfg.n_inner, t % cfg.n_inner

    in_specs = [
        pl.BlockSpec((1, T, D_MODEL), lambda i: (*cur(i), 0)),
        pl.BlockSpec((1, T, D_MODEL), lambda i: (*res(i), 0)),
        pl.BlockSpec((1, T, D_PLE), lambda i: (*res(i), 0)),
        pl.BlockSpec((POOL_HIST, hist_blk, D_B), lambda i: (0, (0 if cfg.sequential else cur(i)[0]), 0)),
        _resident(cnt.shape),
    ] + [_resident(w.shape) for w in weights]
    out_shape = [jax.ShapeDtypeStruct(x.shape, jnp.float32),
                 jax.ShapeDtypeStruct((POOL_HIST, n_hist_cols, D_B), jnp.float32)]
    out_specs = [pl.BlockSpec((1, T, D_MODEL), lambda i: (*res(i), 0)),
                 pl.BlockSpec((POOL_HIST, hist_blk, D_B),
                              lambda i: (0, (cur(i)[0] // HIST_OUT_SEQS if cfg.sequential else cur(i)[0]), 0))]
    if cfg.emit_v:
        out_shape.append(jax.ShapeDtypeStruct(x.shape[:2] + (D_A,), jnp.float32))
        out_specs.append(pl.BlockSpec((1, T, D_A), lambda i: (*cur(i), 0)))
    return pl.pallas_call(
        functools.partial(_layer_kernel, cfg),
        grid=(cfg.n_tiles + (1 if cfg.pipelined else 0),),
        in_specs=in_specs,
        out_specs=out_specs,
        out_shape=out_shape,
        scratch_shapes=[pltpu.VMEM((cfg.n_sub, HIST_ROWS + cfg.sub_len, D_B), jnp.float32),
                        pltpu.VMEM((T, D_MIX), jnp.bfloat16),
                        pltpu.VMEM((T, D_MIX), jnp.bfloat16)],
        compiler_params=pltpu.CompilerParams(dimension_semantics=("arbitrary",),
                                             vmem_limit_bytes=VMEM_LIMIT_BYTES),
        name="layer_sample" if cfg.emit_v else "layer_prompt",
    )(x, x, p, hist, cnt, *weights)


def kernel(x_prompt, x_sample, state_pool, p_prompt, p_sample, w_in, ln_v_g, ln_v_b, w_s, b_s, pool_w, pool_scale, w_out, b_out, ln_g, ln_b, w_ple, w_pg, b_pg):
    assert DEPTH == 1 and w_in.shape[0] == DEPTH
    B, S, _ = x_prompt.shape
    Bs, Ss, _ = x_sample.shape
    bf = jnp.bfloat16
    row = lambda a: a.reshape(1, -1).astype(jnp.float32)

    blk = np.arange(MLP_CHUNK) // CHUNK
    mask = jnp.asarray(blk[None, :] <= blk[:, None])
    ws_masked = jnp.where(mask[None], w_s[0], jnp.zeros((), w_s.dtype)).astype(bf)
    bs_tbl = jnp.repeat(jnp.transpose(b_s[0]), HEAD_A, axis=1).astype(jnp.float32)
    win = np.repeat(np.asarray(POOL_WINDOWS, np.float32), POOL_GROUP)[None, :]
    cnt = jnp.asarray(np.minimum(np.arange(1, HIST_ROWS + 1, dtype=np.float32)[:, None], win))
    vec_rows = [row(ln_v_g[0]), row(ln_v_b[0]), row(pool_scale[0]), row(b_out[0]), row(ln_g[0]), row(ln_b[0]),
                row(b_pg[0])]
    assert len(vec_rows) == N_VECS
    vecs = jnp.concatenate(vec_rows + [jnp.zeros((VEC_ROWS - N_VECS, D_MODEL), jnp.float32)], axis=0)
    weights = (vecs, w_in[0].astype(bf), ws_masked, bs_tbl, pool_w[0].astype(bf), w_out[0].astype(bf),
               w_ple[0].astype(bf), w_pg[0].astype(bf))

    n_inner = S // PROMPT_TILE_ROWS
    cfg_p = TileConfig(n_sub=1, sub_len=PROMPT_TILE_ROWS, n_inner=n_inner, n_tiles=B * n_inner,
                       sequential=True, pipelined=True, emit_v=False)
    hist_p = jnp.zeros((POOL_HIST, HIST_OUT_SEQS, D_B), jnp.float32)
    y_p, hist_p_new = _run_layer(cfg_p, x_prompt, p_prompt[0], hist_p, cnt, weights)

    assert PAST_LEN >= POOL_HIST
    ns = SAMPLE_STREAMS_PER_TILE
    cfg_s = TileConfig(n_sub=ns, sub_len=Ss, n_inner=1, n_tiles=Bs // ns,
                       sequential=False, pipelined=False, emit_v=True)
    hist_s = jnp.transpose(state_pool[0], (1, 0, 2))
    y_s, hist_s_new, v_s = _run_layer(
        cfg_s, x_sample.reshape(Bs // ns, ns * Ss, D_MODEL),
        p_sample[0].reshape(Bs // ns, ns * Ss, D_PLE), hist_s, cnt, weights)

    return (y_p,
            y_s.reshape(Bs, Ss, D_MODEL),
            jnp.transpose(hist_p_new, (1, 0, 2))[None],
            jnp.transpose(hist_s_new, (1, 0, 2))[None],
            v_s.reshape(1, Bs, Ss, N_HEADS_A, HEAD_A))
```

```python
import functools
from typing import NamedTuple

import numpy as np
import jax
import jax.numpy as jnp
from jax.experimental import pallas as pl
from jax.experimental.pallas import tpu as pltpu

D_MODEL = 1024
DEPTH = 1
PAST_LEN = 4096
CHUNK = 64
MLP_CHUNK = 128
D_MIX = 2 * D_MODEL
D_A = D_MIX // 2
HEAD_A = 128
N_HEADS_A = D_A // HEAD_A
D_B = D_MIX - D_A
POOL_WINDOWS = (2, 4, 8, 16)
N_POOL_GROUPS = len(POOL_WINDOWS)
POOL_GROUP = D_B // N_POOL_GROUPS
POOL_HIST = max(POOL_WINDOWS) - 1
HIST_ROWS = POOL_HIST + 1
N_VECS, VEC_ROWS = 7, 8
D_PLE = 256
D_IN = 3 * D_A + 2 * D_B
ALPHA = (2.0 * DEPTH) ** 0.25
LN_EPS = 1e-5
STAGE_ROWS, STAGE_COLS = 256, 1024
STAGE_SLOTS = 8
BIG_WEIGHT_SHAPES = ((D_MODEL, D_IN), (D_MIX, D_MODEL), (D_PLE, D_MODEL), (D_MODEL, D_MODEL))
assert all(k % STAGE_ROWS == 0 and n % STAGE_COLS == 0 for k, n in BIG_WEIGHT_SHAPES)

PROMPT_TILE_ROWS = 512
SAMPLE_STREAMS_PER_TILE = 8
VMEM_LIMIT_BYTES = 56 * 1024 * 1024
COL_PIECE = 256
N_COL_PIECES = D_MODEL // COL_PIECE
assert COL_PIECE == POOL_GROUP and COL_PIECE == 2 * HEAD_A


def _numbered(name, n):
    return tuple(f"{name}{i}" for i in range(n))


GA_DOTS = N_COL_PIECES
GATE_DOTS = N_COL_PIECES
STAGE1_PIECES = (("v_proj", "u_proj") + _numbered("ga_proj", GA_DOTS) + _numbered("ln_v", N_HEADS_A)
                 + _numbered("a_out", N_COL_PIECES) + ("hp_proj", "gb_proj") + _numbered("pooling", N_POOL_GROUPS)
                 + _numbered("b_out", N_POOL_GROUPS))
STAGE2_PIECES = ("w_out", "post_norm") + _numbered("gate", GATE_DOTS)
PIPELINE_ORDER = (
    "v_proj", "w_out", "u_proj",
    "ln_v0", "ln_v1", "ga_proj0", "ln_v2", "ln_v3", "ga_proj1",
    "ln_v4", "ln_v5", "ga_proj2", "ln_v6", "ln_v7", "ga_proj3",
    "post_norm", "a_out0", "a_out1", "a_out2", "a_out3", "hp_proj",
    "pooling0", "pooling1", "pooling2", "pooling3",
    "gate0", "gate1", "gate2", "gate3", "gb_proj",
    "b_out0", "b_out1", "b_out2", "b_out3")
assert sorted(PIPELINE_ORDER) == sorted(STAGE1_PIECES + STAGE2_PIECES)


class TileConfig(NamedTuple):
    n_sub: int
    sub_len: int
    n_inner: int
    n_tiles: int
    sequential: bool
    pipelined: bool
    emit_v: bool

    @property
    def tile_rows(self):
        return self.n_sub * self.sub_len


def _layernorm(x, g, b):
    mu = jnp.mean(x, axis=-1, keepdims=True)
    xc = x - mu
    var = jnp.mean(xc * xc, axis=-1, keepdims=True)
    return xc * jax.lax.rsqrt(var + LN_EPS) * g + b


def _dot(a, b):
    return jnp.dot(a, b, preferred_element_type=jnp.float32)


def _history_to_rows(z_ref, s, row0, hist_ref, col):
    for r in range(POOL_HIST):
        z_ref[s, row0 + r:row0 + r + 1, :] = hist_ref[r, col:col + 1, :]


def _rows_to_history(hist_out_ref, col, z_ref, s, row0):
    for r in range(POOL_HIST):
        hist_out_ref[r, col:col + 1, :] = z_ref[s, row0 + r:row0 + r + 1, :]


def _stage_weights(hbm_refs, vmem_refs, stage_ref, sem):
    n_slots = stage_ref.shape[0]
    chunks = [(src, dst, r, c)
              for src, dst in zip(hbm_refs, vmem_refs)
              for r in range(0, src.shape[0], STAGE_ROWS)
              for c in range(0, src.shape[1], STAGE_COLS)]

    def copy(k):
        src, _, r, c = chunks[k]
        return pltpu.make_async_copy(src.at[pl.ds(r, STAGE_ROWS), pl.ds(c, STAGE_COLS)],
                                     stage_ref.at[k % n_slots], sem.at[k % n_slots])

    for k in range(min(n_slots - 1, len(chunks))):
        copy(k).start()
    for k, (_, dst, r, c) in enumerate(chunks):
        if k + n_slots - 1 < len(chunks):
            copy(k + n_slots - 1).start()
        copy(k).wait()
        dst[r:r + STAGE_ROWS, c:c + STAGE_COLS] = stage_ref[k % n_slots].astype(jnp.bfloat16)


def _layer_kernel(cfg, x_ref, xres_ref, p_ref, hist_ref, cnt_ref, vecs_ref, ws_ref, bs_ref, pool_w_ref,
                  w_in_hbm, w_out_hbm, w_ple_hbm, w_pg_hbm, *out_and_scratch):
    *out_and_scratch, w_in_ref, w_out_ref, w_ple_ref, w_pg_ref, stage_ref, stage_sem = out_and_scratch
    lnv_g_ref, lnv_b_ref, pool_scale_ref, b_out_ref, ln_g_ref, ln_b_ref, b_pg_ref = (
        vecs_ref.at[k:k + 1] for k in range(N_VECS))
    if cfg.emit_v:
        y_ref, hist_out_ref, v_out_ref, z_ref, mix_ref, cat_ref = out_and_scratch
    else:
        y_ref, hist_out_ref, z_ref, mix_ref, cat_ref = out_and_scratch
        v_out_ref = None
    L = cfg.sub_len
    last_sub = cfg.n_sub - 1
    step = pl.program_id(0)
    j = jax.lax.rem(jnp.minimum(step, cfg.n_tiles - 1), cfg.n_inner)

    @pl.when(step == 0)
    def _():
        _stage_weights((w_in_hbm, w_out_hbm, w_ple_hbm, w_pg_hbm), (w_in_ref, w_out_ref, w_ple_ref, w_pg_ref),
                       stage_ref, stage_sem)
        if cfg.pipelined:
            cat_ref[...] = jnp.zeros_like(cat_ref)

    if cfg.sequential:
        @pl.when(j == 0)
        def _():
            z_ref[last_sub, L:, :] = hist_ref[0]

    val = {}
    pieces = {}

    def piece(name):
        def register(fn):
            pieces[name] = fn
        return register

    def col_piece(n, base=0):
        return slice(base + n * COL_PIECE, base + (n + 1) * COL_PIECE)

    def proj(lo, hi):
        return _dot(val["xb"], w_in_ref[:, lo:hi])

    @piece("v_proj")
    def _():
        val["xb"] = x_ref[0].astype(jnp.bfloat16)
        val["v"] = proj(D_A, 2 * D_A)

    @piece("u_proj")
    def _():
        val["u"] = proj(0, D_A)

    ga_width = D_A // GA_DOTS
    for n in range(GA_DOTS):
        @piece(f"ga_proj{n}")
        def _(n=n):
            val[f"ga{n}"] = proj(2 * D_A + n * ga_width, 2 * D_A + (n + 1) * ga_width)

    @piece("hp_proj")
    def _():
        hp = val["hp"] = proj(3 * D_A, 3 * D_A + D_B)
        if cfg.sequential:
            z_ref[0, :HIST_ROWS, :] = z_ref[last_sub, L:, :]
        else:
            z_ref[:, :1, :] = jnp.zeros((cfg.n_sub, 1, D_B), jnp.float32)
            for s in range(cfg.n_sub):
                _history_to_rows(z_ref, s, 1, hist_ref, s)
        for s in range(cfg.n_sub):
            if cfg.sequential and s > 0:
                z_ref[s, :HIST_ROWS, :] = z_ref[s - 1, L:, :]
            z_ref[s, HIST_ROWS:, :] = hp[s * L:(s + 1) * L]
        if cfg.sequential:
            hist_out_ref[0] = z_ref[last_sub, L:, :]
        else:
            for s in range(cfg.n_sub):
                _rows_to_history(hist_out_ref, s, z_ref, s, L + 1)

    @piece("gb_proj")
    def _():
        val["gb"] = proj(3 * D_A + D_B, D_IN)

    for h in range(N_HEADS_A):
        @piece(f"ln_v{h}")
        def _(h=h):
            sl = slice(h * HEAD_A, (h + 1) * HEAD_A)
            vn = _layernorm(val["v"][:, sl], lnv_g_ref[:, sl], lnv_b_ref[:, sl])
            if v_out_ref is not None:
                v_out_ref[0, :, sl] = vn
            vnb = vn.astype(jnp.bfloat16)
            for s in range(cfg.n_sub):
                for c0 in range(0, L, MLP_CHUNK):
                    lc = min(MLP_CHUNK, L - c0)
                    r0 = s * L + c0
                    mixed = _dot(ws_ref[h, :lc, :lc], vnb[r0:r0 + lc])
                    mix_ref[r0:r0 + lc, sl] = (mixed + bs_ref[:lc, sl]).astype(jnp.bfloat16)

    for n in range(N_COL_PIECES):
        @piece(f"a_out{n}")
        def _(n=n):
            cols = col_piece(n)
            k, off = divmod(n * COL_PIECE, ga_width)
            ga = val[f"ga{k}"][:, off:off + COL_PIECE]
            a_out = val["u"][:, cols] * mix_ref[:, cols].astype(jnp.float32) * jax.nn.silu(ga)
            cat_ref[:, cols] = a_out.astype(jnp.bfloat16)

    for g, w in enumerate(POOL_WINDOWS):
        @piece(f"pooling{g}")
        def _(g=g, w=w):
            cols, ccols = col_piece(g), col_piece(g, D_A)
            for s in range(cfg.n_sub):
                rows = slice(s * L, (s + 1) * L)
                acc = z_ref[s, :, cols]
                span = 1
                while span < w:
                    acc = acc + pltpu.roll(acc, span, axis=0)
                    span *= 2
                win = acc[HIST_ROWS:]
                hp_g = val["hp"][rows, cols]
                mix_ref[rows, ccols] = (win * (1.0 / w) - hp_g).astype(jnp.bfloat16)
                if cfg.sequential and s == 0:
                    cnt = jnp.where(j == 0, cnt_ref[:, cols], float(w))
                    head = win[:HIST_ROWS] / cnt - hp_g[:HIST_ROWS]
                    mix_ref[:HIST_ROWS, ccols] = head.astype(jnp.bfloat16)

        @piece(f"b_out{g}")
        def _(g=g):
            cols, ccols = col_piece(g), col_piece(g, D_A)
            q = _dot(mix_ref[:, ccols], pool_w_ref[g])
            b_outp = q * pool_scale_ref[:, cols] * jax.nn.silu(val["gb"][:, cols])
            cat_ref[:, ccols] = b_outp.astype(jnp.bfloat16)

    @piece("w_out")
    def _():
        y = _dot(cat_ref[...], w_out_ref[...]) + b_out_ref[...]
        val["pre"] = ALPHA * xres_ref[0] + y

    @piece("post_norm")
    def _():
        xn = val["xn"] = _layernorm(val["pre"], ln_g_ref[...], ln_b_ref[...])
        val["xnb"] = xn.astype(jnp.bfloat16)
        val["pb"] = p_ref[0].astype(jnp.bfloat16)

    gate_width = D_MODEL // GATE_DOTS
    for n in range(GATE_DOTS):
        @piece(f"gate{n}")
        def _(n=n):
            cols = slice(n * gate_width, (n + 1) * gate_width)
            gate = jax.nn.sigmoid(_dot(val["xnb"], w_pg_ref[:, cols]) + b_pg_ref[:, cols])
            ple = _dot(val["pb"], w_ple_ref[:, cols])
            y_ref[0, :, cols] = val["xn"][:, cols] + gate * ple

    assert set(pieces) == set(STAGE1_PIECES + STAGE2_PIECES)
    for name in (PIPELINE_ORDER if cfg.pipelined else STAGE1_PIECES + STAGE2_PIECES):
        pieces[name]()


def _resident(shape):
    nd = len(shape)
    return pl.BlockSpec(shape, lambda i: (0,) * nd, pipeline_mode=pl.Buffered(1))


def _run_layer(cfg, x, p, hist, cnt, weights, big):
    T = cfg.tile_rows
    n_seq = cfg.n_tiles // cfg.n_inner
    assert x.shape == (n_seq, cfg.n_inner * T, D_MODEL) and p.shape == (n_seq, cfg.n_inner * T, D_PLE)
    assert cfg.sequential or cfg.n_inner == 1
    if cfg.sequential:
        assert hist.shape == (n_seq, HIST_ROWS, D_B)
        hist_spec = pl.BlockSpec((1, HIST_ROWS, D_B), lambda i: (cur(i)[0], 0, 0))
    else:
        assert hist.shape == (POOL_HIST, cfg.n_tiles * cfg.n_sub, D_B) and cfg.n_sub % 8 == 0
        hist_spec = pl.BlockSpec((POOL_HIST, cfg.n_sub, D_B), lambda i: (0, cur(i)[0], 0))

    def cur(i):
        t = jnp.minimum(i, cfg.n_tiles - 1)
        return t // cfg.n_inner, t % cfg.n_inner

    def res(i):
        t = jnp.maximum(i - 1, 0) if cfg.pipelined else i
        return t // cfg.n_inner, t % cfg.n_inner

    in_specs = [
        pl.BlockSpec((1, T, D_MODEL), lambda i: (*cur(i), 0)),
        pl.BlockSpec((1, T, D_MODEL), lambda i: (*res(i), 0)),
        pl.BlockSpec((1, T, D_PLE), lambda i: (*res(i), 0)),
        hist_spec,
        _resident(cnt.shape),
    ] + [_resident(w.shape) for w in weights] + [pl.BlockSpec(memory_space=pl.ANY) for _ in big]
    out_shape = [jax.ShapeDtypeStruct(x.shape, jnp.float32),
                 jax.ShapeDtypeStruct(hist.shape, jnp.float32)]
    out_specs = [pl.BlockSpec((1, T, D_MODEL), lambda i: (*res(i), 0)), hist_spec]
    if cfg.emit_v:
        out_shape.append(jax.ShapeDtypeStruct(x.shape[:2] + (D_A,), jnp.float32))
        out_specs.append(pl.BlockSpec((1, T, D_A), lambda i: (*cur(i), 0)))
    return pl.pallas_call(
        functools.partial(_layer_kernel, cfg),
        grid=(cfg.n_tiles + (1 if cfg.pipelined else 0),),
        in_specs=in_specs,
        out_specs=out_specs,
        out_shape=out_shape,
        scratch_shapes=[pltpu.VMEM((cfg.n_sub, HIST_ROWS + cfg.sub_len, D_B), jnp.float32),
                        pltpu.VMEM((T, D_MIX), jnp.bfloat16),
                        pltpu.VMEM((T, D_MIX), jnp.bfloat16)]
                       + [pltpu.VMEM(s, jnp.bfloat16) for s in BIG_WEIGHT_SHAPES]
                       + [pltpu.VMEM((STAGE_SLOTS, STAGE_ROWS, STAGE_COLS), jnp.float32),
                          pltpu.SemaphoreType.DMA((STAGE_SLOTS,))],
        compiler_params=pltpu.CompilerParams(dimension_semantics=("arbitrary",),
                                             vmem_limit_bytes=VMEM_LIMIT_BYTES),
        name="layer_sample" if cfg.emit_v else "layer_prompt",
    )(x, x, p, hist, cnt, *weights, *big)


def kernel(x_prompt, x_sample, state_pool, p_prompt, p_sample, w_in, ln_v_g, ln_v_b, w_s, b_s, pool_w, pool_scale, w_out, b_out, ln_g, ln_b, w_ple, w_pg, b_pg):
    assert DEPTH == 1 and w_in.shape[0] == DEPTH
    B, S, _ = x_prompt.shape
    Bs, Ss, _ = x_sample.shape
    bf = jnp.bfloat16
    row = lambda a: a.reshape(1, -1).astype(jnp.float32)

    blk = np.arange(MLP_CHUNK) // CHUNK
    mask = jnp.asarray(blk[None, :] <= blk[:, None])
    ws_masked = jnp.where(mask[None], w_s[0], jnp.zeros((), w_s.dtype)).astype(bf)
    bs_tbl = jnp.repeat(jnp.transpose(b_s[0]), HEAD_A, axis=1).astype(jnp.float32)
    win = np.repeat(np.asarray(POOL_WINDOWS, np.float32), POOL_GROUP)[None, :]
    cnt = jnp.asarray(np.minimum(np.arange(1, HIST_ROWS + 1, dtype=np.float32)[:, None], win))
    vec_rows = [row(ln_v_g[0]), row(ln_v_b[0]), row(pool_scale[0]), row(b_out[0]), row(ln_g[0]), row(ln_b[0]),
                row(b_pg[0])]
    assert len(vec_rows) == N_VECS
    vecs = jnp.concatenate(vec_rows + [jnp.zeros((VEC_ROWS - N_VECS, D_MODEL), jnp.float32)], axis=0)
    weights = (vecs, ws_masked, bs_tbl, pool_w[0].astype(bf))
    big = (w_in[0], w_out[0], w_ple[0], w_pg[0])
    assert tuple(w.shape for w in big) == BIG_WEIGHT_SHAPES and all(w.dtype == jnp.float32 for w in big)

    n_inner = S // PROMPT_TILE_ROWS
    cfg_p = TileConfig(n_sub=1, sub_len=PROMPT_TILE_ROWS, n_inner=n_inner, n_tiles=B * n_inner,
                       sequential=True, pipelined=True, emit_v=False)
    hist_p = jnp.zeros((B, HIST_ROWS, D_B), jnp.float32)
    y_p, hist_p_new = _run_layer(cfg_p, x_prompt, p_prompt[0], hist_p, cnt, weights, big)

    assert PAST_LEN >= POOL_HIST
    ns = SAMPLE_STREAMS_PER_TILE
    cfg_s = TileConfig(n_sub=ns, sub_len=Ss, n_inner=1, n_tiles=Bs // ns,
                       sequential=False, pipelined=False, emit_v=True)
    hist_s = jnp.transpose(state_pool[0], (1, 0, 2))
    y_s, hist_s_new, v_s = _run_layer(
        cfg_s, x_sample.reshape(Bs // ns, ns * Ss, D_MODEL),
        p_sample[0].reshape(Bs // ns, ns * Ss, D_PLE), hist_s, cnt, weights, big)

    return (y_p,
            y_s.reshape(Bs, Ss, D_MODEL),
            hist_p_new[None, :, 1:, :],
            jnp.transpose(hist_s_new, (1, 0, 2))[None],
            v_s.reshape(1, Bs, Ss, N_HEADS_A, HEAD_A))
```

```python
import functools
from typing import NamedTuple

import numpy as np
import jax
import jax.numpy as jnp
from jax.experimental import pallas as pl
from jax.experimental.pallas import tpu as pltpu

D_MODEL = 1024
DEPTH = 1
PAST_LEN = 4096
CHUNK = 64
MLP_CHUNK = 128
D_MIX = 2 * D_MODEL
D_A = D_MIX // 2
HEAD_A = 128
N_HEADS_A = D_A // HEAD_A
D_B = D_MIX - D_A
POOL_WINDOWS = (2, 4, 8, 16)
N_POOL_GROUPS = len(POOL_WINDOWS)
POOL_GROUP = D_B // N_POOL_GROUPS
POOL_HIST = max(POOL_WINDOWS) - 1
HIST_ROWS = POOL_HIST + 1
N_VECS, VEC_ROWS = 7, 8
D_PLE = 256
D_IN = 3 * D_A + 2 * D_B
ALPHA = (2.0 * DEPTH) ** 0.25
LN_EPS = 1e-5
STAGE_ROWS, STAGE_COLS = 256, 1024
STAGE_SLOTS = 8
BIG_WEIGHT_SHAPES = ((D_MODEL, D_IN), (D_MIX, D_MODEL), (D_PLE, D_MODEL), (D_MODEL, D_MODEL))
assert all(k % STAGE_ROWS == 0 and n % STAGE_COLS == 0 for k, n in BIG_WEIGHT_SHAPES)

PROMPT_TILE_ROWS = 512
SAMPLE_STREAMS_PER_TILE = 8
VMEM_LIMIT_BYTES = 56 * 1024 * 1024
COL_PIECE = 256
N_COL_PIECES = D_MODEL // COL_PIECE
assert COL_PIECE == POOL_GROUP and COL_PIECE == 2 * HEAD_A


def _numbered(name, n):
    return tuple(f"{name}{i}" for i in range(n))


GA_DOTS = N_COL_PIECES
GATE_DOTS = N_COL_PIECES
STAGE1_PIECES = (("v_proj", "u_proj") + _numbered("ga_proj", GA_DOTS) + _numbered("ln_v", N_HEADS_A)
                 + _numbered("a_out", N_COL_PIECES) + ("hp_proj", "gb_proj") + _numbered("pooling", N_POOL_GROUPS)
                 + _numbered("b_out", N_POOL_GROUPS))
STAGE2_PIECES = ("w_out", "post_norm") + _numbered("gate", GATE_DOTS)
PIPELINE_ORDER = (
    "v_proj", "w_out", "u_proj",
    "ln_v0", "ln_v1", "ga_proj0", "ln_v2", "ln_v3", "ga_proj1",
    "ln_v4", "ln_v5", "ga_proj2", "ln_v6", "ln_v7", "ga_proj3",
    "post_norm", "a_out0", "a_out1", "a_out2", "a_out3",
    "gate0", "gate1", "gate2", "gate3", "hp_proj",
    "pooling0", "pooling1", "pooling2", "pooling3", "gb_proj",
    "b_out0", "b_out1", "b_out2", "b_out3")
assert sorted(PIPELINE_ORDER) == sorted(STAGE1_PIECES + STAGE2_PIECES)


class TileConfig(NamedTuple):
    n_sub: int
    sub_len: int
    n_inner: int
    n_tiles: int
    sequential: bool
    pipelined: bool
    emit_v: bool

    @property
    def tile_rows(self):
        return self.n_sub * self.sub_len


def _layernorm(x, g, b):
    mu = jnp.mean(x, axis=-1, keepdims=True)
    xc = x - mu
    var = jnp.mean(xc * xc, axis=-1, keepdims=True)
    return xc * jax.lax.rsqrt(var + LN_EPS) * g + b


def _dot(a, b):
    return jnp.dot(a, b, preferred_element_type=jnp.float32)


def _history_to_rows(z_ref, s, row0, hist_ref, col):
    for r in range(POOL_HIST):
        z_ref[s, row0 + r:row0 + r + 1, :] = hist_ref[r, col:col + 1, :]


def _rows_to_history(hist_out_ref, col, z_ref, s, row0):
    for r in range(POOL_HIST):
        hist_out_ref[r, col:col + 1, :] = z_ref[s, row0 + r:row0 + r + 1, :]


def _stage_weights(hbm_refs, vmem_refs, stage_ref, sem):
    n_slots = stage_ref.shape[0]
    chunks = [(src, dst, r, c)
              for src, dst in zip(hbm_refs, vmem_refs)
              for r in range(0, src.shape[0], STAGE_ROWS)
              for c in range(0, src.shape[1], STAGE_COLS)]

    def copy(k):
        src, _, r, c = chunks[k]
        return pltpu.make_async_copy(src.at[pl.ds(r, STAGE_ROWS), pl.ds(c, STAGE_COLS)],
                                     stage_ref.at[k % n_slots], sem.at[k % n_slots])

    for k in range(min(n_slots - 1, len(chunks))):
        copy(k).start()
    for k, (_, dst, r, c) in enumerate(chunks):
        if k + n_slots - 1 < len(chunks):
            copy(k + n_slots - 1).start()
        copy(k).wait()
        dst[r:r + STAGE_ROWS, c:c + STAGE_COLS] = stage_ref[k % n_slots].astype(jnp.bfloat16)


def _layer_kernel(cfg, x_ref, xres_ref, p_ref, hist_ref, cnt_ref, vecs_ref, ws_ref, bs_ref, pool_w_ref,
                  w_in_hbm, w_out_hbm, w_ple_hbm, w_pg_hbm, *out_and_scratch):
    *out_and_scratch, w_in_ref, w_out_ref, w_ple_ref, w_pg_ref, stage_ref, stage_sem = out_and_scratch
    lnv_g_ref, lnv_b_ref, pool_scale_ref, b_out_ref, ln_g_ref, ln_b_ref, b_pg_ref = (
        vecs_ref.at[k:k + 1] for k in range(N_VECS))
    if cfg.emit_v:
        y_ref, hist_out_ref, v_out_ref, z_ref, mix_ref, cat_ref = out_and_scratch
    else:
        y_ref, hist_out_ref, z_ref, mix_ref, cat_ref = out_and_scratch
        v_out_ref = None
    L = cfg.sub_len
    last_sub = cfg.n_sub - 1
    step = pl.program_id(0)
    j = jax.lax.rem(jnp.minimum(step, cfg.n_tiles - 1), cfg.n_inner)

    @pl.when(step == 0)
    def _():
        _stage_weights((w_in_hbm, w_out_hbm, w_ple_hbm, w_pg_hbm), (w_in_ref, w_out_ref, w_ple_ref, w_pg_ref),
                       stage_ref, stage_sem)
        if cfg.pipelined:
            cat_ref[...] = jnp.zeros_like(cat_ref)

    if cfg.sequential:
        @pl.when(j == 0)
        def _():
            z_ref[last_sub, L:, :] = hist_ref[0]

    val = {}
    pieces = {}

    def piece(name):
        def register(fn):
            pieces[name] = fn
        return register

    def col_piece(n, base=0):
        return slice(base + n * COL_PIECE, base + (n + 1) * COL_PIECE)

    def proj(lo, hi):
        return _dot(val["xb"], w_in_ref[:, lo:hi])

    @piece("v_proj")
    def _():
        val["xb"] = x_ref[0].astype(jnp.bfloat16)
        val["v"] = proj(D_A, 2 * D_A)

    @piece("u_proj")
    def _():
        val["u"] = proj(0, D_A)

    ga_width = D_A // GA_DOTS
    for n in range(GA_DOTS):
        @piece(f"ga_proj{n}")
        def _(n=n):
            val[f"ga{n}"] = proj(2 * D_A + n * ga_width, 2 * D_A + (n + 1) * ga_width)

    @piece("hp_proj")
    def _():
        hp = val["hp"] = proj(3 * D_A, 3 * D_A + D_B)
        if cfg.sequential:
            z_ref[0, :HIST_ROWS, :] = z_ref[last_sub, L:, :]
        else:
            z_ref[:, :1, :] = jnp.zeros((cfg.n_sub, 1, D_B), jnp.float32)
            for s in range(cfg.n_sub):
                _history_to_rows(z_ref, s, 1, hist_ref, s)
        for s in range(cfg.n_sub):
            if cfg.sequential and s > 0:
                z_ref[s, :HIST_ROWS, :] = z_ref[s - 1, L:, :]
            z_ref[s, HIST_ROWS:, :] = hp[s * L:(s + 1) * L]
        if cfg.sequential:
            hist_out_ref[0] = z_ref[last_sub, L:, :]
        else:
            for s in range(cfg.n_sub):
                _rows_to_history(hist_out_ref, s, z_ref, s, L + 1)

    @piece("gb_proj")
    def _():
        val["gb"] = proj(3 * D_A + D_B, D_IN)

    for h in range(N_HEADS_A):
        @piece(f"ln_v{h}")
        def _(h=h):
            sl = slice(h * HEAD_A, (h + 1) * HEAD_A)
            vn = _layernorm(val["v"][:, sl], lnv_g_ref[:, sl], lnv_b_ref[:, sl])
            if v_out_ref is not None:
                v_out_ref[0, :, sl] = vn
            vnb = vn.astype(jnp.bfloat16)
            for s in range(cfg.n_sub):
                for c0 in range(0, L, MLP_CHUNK):
                    lc = min(MLP_CHUNK, L - c0)
                    r0 = s * L + c0
                    mixed = _dot(ws_ref[h, :lc, :lc], vnb[r0:r0 + lc])
                    mix_ref[r0:r0 + lc, sl] = (mixed + bs_ref[:lc, sl]).astype(jnp.bfloat16)

    for n in range(N_COL_PIECES):
        @piece(f"a_out{n}")
        def _(n=n):
            cols = col_piece(n)
            k, off = divmod(n * COL_PIECE, ga_width)
            ga = val[f"ga{k}"][:, off:off + COL_PIECE]
            a_out = val["u"][:, cols] * mix_ref[:, cols].astype(jnp.float32) * jax.nn.silu(ga)
            cat_ref[:, cols] = a_out.astype(jnp.bfloat16)

    for g, w in enumerate(POOL_WINDOWS):
        @piece(f"pooling{g}")
        def _(g=g, w=w):
            cols, ccols = col_piece(g), col_piece(g, D_A)
            for s in range(cfg.n_sub):
                rows = slice(s * L, (s + 1) * L)
                acc = z_ref[s, :, cols]
                span = 1
                while span < w:
                    acc = acc + pltpu.roll(acc, span, axis=0)
                    span *= 2
                win = acc[HIST_ROWS:]
                hp_g = val["hp"][rows, cols]
                mix_ref[rows, ccols] = (win * (1.0 / w) - hp_g).astype(jnp.bfloat16)
                if cfg.sequential and s == 0:
                    cnt = jnp.where(j == 0, cnt_ref[:, cols], float(w))
                    head = win[:HIST_ROWS] / cnt - hp_g[:HIST_ROWS]
                    mix_ref[:HIST_ROWS, ccols] = head.astype(jnp.bfloat16)

        @piece(f"b_out{g}")
        def _(g=g):
            cols, ccols = col_piece(g), col_piece(g, D_A)
            q = _dot(mix_ref[:, ccols], pool_w_ref[g])
            b_outp = q * pool_scale_ref[:, cols] * jax.nn.silu(val["gb"][:, cols])
            cat_ref[:, ccols] = b_outp.astype(jnp.bfloat16)

    @piece("w_out")
    def _():
        y = _dot(cat_ref[...], w_out_ref[...]) + b_out_ref[...]
        val["pre"] = ALPHA * xres_ref[0] + y

    @piece("post_norm")
    def _():
        xn = val["xn"] = _layernorm(val["pre"], ln_g_ref[...], ln_b_ref[...])
        val["xnb"] = xn.astype(jnp.bfloat16)
        val["pb"] = p_ref[0].astype(jnp.bfloat16)

    gate_width = D_MODEL // GATE_DOTS
    for n in range(GATE_DOTS):
        @piece(f"gate{n}")
        def _(n=n):
            cols = slice(n * gate_width, (n + 1) * gate_width)
            gate = jax.nn.sigmoid(_dot(val["xnb"], w_pg_ref[:, cols]) + b_pg_ref[:, cols])
            ple = _dot(val["pb"], w_ple_ref[:, cols])
            y_ref[0, :, cols] = val["xn"][:, cols] + gate * ple

    assert set(pieces) == set(STAGE1_PIECES + STAGE2_PIECES)
    for name in (PIPELINE_ORDER if cfg.pipelined else STAGE1_PIECES + STAGE2_PIECES):
        pieces[name]()


def _resident(shape):
    nd = len(shape)
    return pl.BlockSpec(shape, lambda i: (0,) * nd, pipeline_mode=pl.Buffered(1))


def _run_layer(cfg, x, p, hist, cnt, weights, big):
    T = cfg.tile_rows
    n_seq = cfg.n_tiles // cfg.n_inner
    assert x.shape == (n_seq, cfg.n_inner * T, D_MODEL) and p.shape == (n_seq, cfg.n_inner * T, D_PLE)
    assert cfg.sequential or cfg.n_inner == 1
    if cfg.sequential:
        assert hist.shape == (n_seq, HIST_ROWS, D_B)
        hist_spec = pl.BlockSpec((1, HIST_ROWS, D_B), lambda i: (cur(i)[0], 0, 0))
    else:
        assert hist.shape == (POOL_HIST, cfg.n_tiles * cfg.n_sub, D_B) and cfg.n_sub % 8 == 0
        hist_spec = pl.BlockSpec((POOL_HIST, cfg.n_sub, D_B), lambda i: (0, cur(i)[0], 0))

    def cur(i):
        t = jnp.minimum(i, cfg.n_tiles - 1)
        return t // cfg.n_inner, t % cfg.n_inner

    def res(i):
        t = jnp.maximum(i - 1, 0) if cfg.pipelined else i
        return t // cfg.n_inner, t % cfg.n_inner

    in_specs = [
        pl.BlockSpec((1, T, D_MODEL), lambda i: (*cur(i), 0)),
        pl.BlockSpec((1, T, D_MODEL), lambda i: (*res(i), 0)),
        pl.BlockSpec((1, T, D_PLE), lambda i: (*res(i), 0)),
        hist_spec,
        _resident(cnt.shape),
    ] + [_resident(w.shape) for w in weights] + [pl.BlockSpec(memory_space=pl.ANY) for _ in big]
    out_shape = [jax.ShapeDtypeStruct(x.shape, jnp.float32),
                 jax.ShapeDtypeStruct(hist.shape, jnp.float32)]
    out_specs = [pl.BlockSpec((1, T, D_MODEL), lambda i: (*res(i), 0)), hist_spec]
    if cfg.emit_v:
        out_shape.append(jax.ShapeDtypeStruct(x.shape[:2] + (D_A,), jnp.float32))
        out_specs.append(pl.BlockSpec((1, T, D_A), lambda i: (*cur(i), 0)))
    return pl.pallas_call(
        functools.partial(_layer_kernel, cfg),
        grid=(cfg.n_tiles + (1 if cfg.pipelined else 0),),
        in_specs=in_specs,
        out_specs=out_specs,
        out_shape=out_shape,
        scratch_shapes=[pltpu.VMEM((cfg.n_sub, HIST_ROWS + cfg.sub_len, D_B), jnp.float32),
                        pltpu.VMEM((T, D_MIX), jnp.bfloat16),
                        pltpu.VMEM((T, D_MIX), jnp.bfloat16)]
                       + [pltpu.VMEM(s, jnp.bfloat16) for s in BIG_WEIGHT_SHAPES]
                       + [pltpu.VMEM((STAGE_SLOTS, STAGE_ROWS, STAGE_COLS), jnp.float32),
                          pltpu.SemaphoreType.DMA((STAGE_SLOTS,))],
        compiler_params=pltpu.CompilerParams(dimension_semantics=("arbitrary",),
                                             vmem_limit_bytes=VMEM_LIMIT_BYTES),
        name="layer_sample" if cfg.emit_v else "layer_prompt",
    )(x, x, p, hist, cnt, *weights, *big)


def kernel(x_prompt, x_sample, state_pool, p_prompt, p_sample, w_in, ln_v_g, ln_v_b, w_s, b_s, pool_w, pool_scale, w_out, b_out, ln_g, ln_b, w_ple, w_pg, b_pg):
    assert DEPTH == 1 and w_in.shape[0] == DEPTH
    B, S, _ = x_prompt.shape
    Bs, Ss, _ = x_sample.shape
    bf = jnp.bfloat16
    row = lambda a: a.reshape(1, -1).astype(jnp.float32)

    blk = np.arange(MLP_CHUNK) // CHUNK
    mask = jnp.asarray(blk[None, :] <= blk[:, None])
    ws_masked = jnp.where(mask[None], w_s[0], jnp.zeros((), w_s.dtype)).astype(bf)
    bs_tbl = jnp.repeat(jnp.transpose(b_s[0]), HEAD_A, axis=1).astype(jnp.float32)
    win = np.repeat(np.asarray(POOL_WINDOWS, np.float32), POOL_GROUP)[None, :]
    cnt = jnp.asarray(np.minimum(np.arange(1, HIST_ROWS + 1, dtype=np.float32)[:, None], win))
    vec_rows = [row(ln_v_g[0]), row(ln_v_b[0]), row(pool_scale[0]), row(b_out[0]), row(ln_g[0]), row(ln_b[0]),
                row(b_pg[0])]
    assert len(vec_rows) == N_VECS
    vecs = jnp.concatenate(vec_rows + [jnp.zeros((VEC_ROWS - N_VECS, D_MODEL), jnp.float32)], axis=0)
    weights = (vecs, ws_masked, bs_tbl, pool_w[0].astype(bf))
    big = (w_in[0], w_out[0], w_ple[0], w_pg[0])
    assert tuple(w.shape for w in big) == BIG_WEIGHT_SHAPES and all(w.dtype == jnp.float32 for w in big)

    n_inner = S // PROMPT_TILE_ROWS
    cfg_p = TileConfig(n_sub=1, sub_len=PROMPT_TILE_ROWS, n_inner=n_inner, n_tiles=B * n_inner,
                       sequential=True, pipelined=True, emit_v=False)
    hist_p = jnp.zeros((B, HIST_ROWS, D_B), jnp.float32)
    y_p, hist_p_new = _run_layer(cfg_p, x_prompt, p_prompt[0], hist_p, cnt, weights, big)

    assert PAST_LEN >= POOL_HIST
    ns = SAMPLE_STREAMS_PER_TILE
    cfg_s = TileConfig(n_sub=ns, sub_len=Ss, n_inner=1, n_tiles=Bs // ns,
                       sequential=False, pipelined=False, emit_v=True)
    hist_s = jnp.transpose(state_pool[0], (1, 0, 2))
    y_s, hist_s_new, v_s = _run_layer(
        cfg_s, x_sample.reshape(Bs // ns, ns * Ss, D_MODEL),
        p_sample[0].reshape(Bs // ns, ns * Ss, D_PLE), hist_s, cnt, weights, big)

    return (y_p,
            y_s.reshape(Bs, Ss, D_MODEL),
            hist_p_new[None, :, 1:, :],
            jnp.transpose(hist_s_new, (1, 0, 2))[None],
            v_s.reshape(1, Bs, Ss, N_HEADS_A, HEAD_A))
```

```python
import functools
from typing import NamedTuple

import numpy as np
import jax
import jax.numpy as jnp
from jax.experimental import pallas as pl
from jax.experimental.pallas import tpu as pltpu

D_MODEL = 1024
DEPTH = 1
PAST_LEN = 4096
CHUNK = 64
MLP_CHUNK = 128
D_MIX = 2 * D_MODEL
D_A = D_MIX // 2
HEAD_A = 128
N_HEADS_A = D_A // HEAD_A
D_B = D_MIX - D_A
POOL_WINDOWS = (2, 4, 8, 16)
N_POOL_GROUPS = len(POOL_WINDOWS)
POOL_GROUP = D_B // N_POOL_GROUPS
POOL_HIST = max(POOL_WINDOWS) - 1
HIST_ROWS = POOL_HIST + 1
N_VECS, VEC_ROWS = 7, 8
D_PLE = 256
D_IN = 3 * D_A + 2 * D_B
ALPHA = (2.0 * DEPTH) ** 0.25
LN_EPS = 1e-5
STAGE_ROWS, STAGE_COLS = 256, 1024
STAGE_SLOTS = 8
BIG_WEIGHT_SHAPES = ((D_MODEL, D_IN), (D_MIX, D_MODEL), (D_PLE, D_MODEL), (D_MODEL, D_MODEL))
assert all(k % STAGE_ROWS == 0 and n % STAGE_COLS == 0 for k, n in BIG_WEIGHT_SHAPES)

PROMPT_TILE_ROWS = 512
SAMPLE_STREAMS_PER_TILE = 8
VMEM_LIMIT_BYTES = 56 * 1024 * 1024
COL_PIECE = 256
N_COL_PIECES = D_MODEL // COL_PIECE
assert COL_PIECE == POOL_GROUP and COL_PIECE == 2 * HEAD_A


def _numbered(name, n):
    return tuple(f"{name}{i}" for i in range(n))


GA_DOTS = N_COL_PIECES
GATE_DOTS = N_COL_PIECES
STAGE1_PIECES = (("v_proj", "u_proj") + _numbered("ga_proj", GA_DOTS) + _numbered("ln_v", N_HEADS_A)
                 + _numbered("a_out", N_COL_PIECES) + ("hp_proj", "gb_proj") + _numbered("pooling", N_POOL_GROUPS)
                 + _numbered("b_out", N_POOL_GROUPS))
STAGE2_PIECES = ("w_out", "post_norm") + _numbered("gate", GATE_DOTS)
PIPELINE_ORDER = (
    "v_proj", "w_out", "u_proj",
    "ln_v0", "ln_v1", "ga_proj0", "ln_v2", "ln_v3", "ga_proj1",
    "ln_v4", "ln_v5", "ga_proj2", "ln_v6", "ln_v7", "ga_proj3",
    "post_norm", "a_out0", "a_out1", "a_out2", "a_out3", "hp_proj",
    "pooling0", "pooling1", "pooling2", "pooling3",
    "gate0", "gate1", "gate2", "gate3", "gb_proj",
    "b_out0", "b_out1", "b_out2", "b_out3")
assert sorted(PIPELINE_ORDER) == sorted(STAGE1_PIECES + STAGE2_PIECES)
assert PIPELINE_ORDER.index("w_out") < PIPELINE_ORDER.index("u_proj")


class TileConfig(NamedTuple):
    n_sub: int
    sub_len: int
    n_inner: int
    n_tiles: int
    sequential: bool
    pipelined: bool
    emit_v: bool

    @property
    def tile_rows(self):
        return self.n_sub * self.sub_len


def _layernorm(x, g, b):
    mu = jnp.mean(x, axis=-1, keepdims=True)
    xc = x - mu
    var = jnp.mean(xc * xc, axis=-1, keepdims=True)
    return xc * jax.lax.rsqrt(var + LN_EPS) * g + b


def _dot(a, b):
    return jnp.dot(a, b, preferred_element_type=jnp.float32)


def _history_to_rows(z_ref, s, row0, hist_ref, col):
    for r in range(POOL_HIST):
        z_ref[s, row0 + r:row0 + r + 1, :] = hist_ref[r, col:col + 1, :]


def _rows_to_history(hist_out_ref, col, z_ref, s, row0):
    for r in range(POOL_HIST):
        hist_out_ref[r, col:col + 1, :] = z_ref[s, row0 + r:row0 + r + 1, :]


def _stage_weights(hbm_refs, vmem_refs, stage_ref, sem):
    n_slots = stage_ref.shape[0]
    chunks = [(src, dst, r, c)
              for src, dst in zip(hbm_refs, vmem_refs)
              for r in range(0, src.shape[0], STAGE_ROWS)
              for c in range(0, src.shape[1], STAGE_COLS)]

    def copy(k):
        src, _, r, c = chunks[k]
        return pltpu.make_async_copy(src.at[pl.ds(r, STAGE_ROWS), pl.ds(c, STAGE_COLS)],
                                     stage_ref.at[k % n_slots], sem.at[k % n_slots])

    for k in range(min(n_slots - 1, len(chunks))):
        copy(k).start()
    for k, (_, dst, r, c) in enumerate(chunks):
        if k + n_slots - 1 < len(chunks):
            copy(k + n_slots - 1).start()
        copy(k).wait()
        dst[r:r + STAGE_ROWS, c:c + STAGE_COLS] = stage_ref[k % n_slots].astype(jnp.bfloat16)


def _layer_kernel(cfg, x_ref, p_ref, hist_ref, cnt_ref, vecs_ref, ws_ref, bs_ref, pool_w_ref,
                  w_in_hbm, w_out_hbm, w_ple_hbm, w_pg_hbm, *out_and_scratch):
    *out_and_scratch, xres_ref, w_in_ref, w_out_ref, w_ple_ref, w_pg_ref, stage_ref, stage_sem = out_and_scratch
    lnv_g_ref, lnv_b_ref, pool_scale_ref, b_out_ref, ln_g_ref, ln_b_ref, b_pg_ref = (
        vecs_ref.at[k:k + 1] for k in range(N_VECS))
    if cfg.emit_v:
        y_ref, hist_out_ref, v_out_ref, z_ref, mix_ref, cat_ref = out_and_scratch
    else:
        y_ref, hist_out_ref, z_ref, mix_ref, cat_ref = out_and_scratch
        v_out_ref = None
    L = cfg.sub_len
    last_sub = cfg.n_sub - 1
    step = pl.program_id(0)
    j = jax.lax.rem(jnp.minimum(step, cfg.n_tiles - 1), cfg.n_inner)

    @pl.when(step == 0)
    def _():
        _stage_weights((w_in_hbm, w_out_hbm, w_ple_hbm, w_pg_hbm), (w_in_ref, w_out_ref, w_ple_ref, w_pg_ref),
                       stage_ref, stage_sem)
        if cfg.pipelined:
            cat_ref[...] = jnp.zeros_like(cat_ref)
            xres_ref[...] = jnp.zeros_like(xres_ref)

    if cfg.sequential:
        @pl.when(j == 0)
        def _():
            z_ref[last_sub, L:, :] = hist_ref[0]

    val = {}
    pieces = {}

    def piece(name):
        def register(fn):
            pieces[name] = fn
        return register

    def col_piece(n, base=0):
        return slice(base + n * COL_PIECE, base + (n + 1) * COL_PIECE)

    def proj(lo, hi):
        return _dot(val["xb"], w_in_ref[:, lo:hi])

    @piece("v_proj")
    def _():
        val["xb"] = x_ref[0].astype(jnp.bfloat16)
        val["v"] = proj(D_A, 2 * D_A)

    @piece("u_proj")
    def _():
        val["u"] = proj(0, D_A)
        xres_ref[...] = x_ref[0]

    ga_width = D_A // GA_DOTS
    for n in range(GA_DOTS):
        @piece(f"ga_proj{n}")
        def _(n=n):
            val[f"ga{n}"] = proj(2 * D_A + n * ga_width, 2 * D_A + (n + 1) * ga_width)

    @piece("hp_proj")
    def _():
        hp = val["hp"] = proj(3 * D_A, 3 * D_A + D_B)
        if cfg.sequential:
            z_ref[0, :HIST_ROWS, :] = z_ref[last_sub, L:, :]
        else:
            z_ref[:, :1, :] = jnp.zeros((cfg.n_sub, 1, D_B), jnp.float32)
            for s in range(cfg.n_sub):
                _history_to_rows(z_ref, s, 1, hist_ref, s)
        for s in range(cfg.n_sub):
            if cfg.sequential and s > 0:
                z_ref[s, :HIST_ROWS, :] = z_ref[s - 1, L:, :]
            z_ref[s, HIST_ROWS:, :] = hp[s * L:(s + 1) * L]
        if cfg.sequential:
            hist_out_ref[0] = z_ref[last_sub, L:, :]
        else:
            for s in range(cfg.n_sub):
                _rows_to_history(hist_out_ref, s, z_ref, s, L + 1)

    @piece("gb_proj")
    def _():
        val["gb"] = proj(3 * D_A + D_B, D_IN)

    for h in range(N_HEADS_A):
        @piece(f"ln_v{h}")
        def _(h=h):
            sl = slice(h * HEAD_A, (h + 1) * HEAD_A)
            vn = _layernorm(val["v"][:, sl], lnv_g_ref[:, sl], lnv_b_ref[:, sl])
            if v_out_ref is not None:
                v_out_ref[0, :, sl] = vn
            vnb = vn.astype(jnp.bfloat16)
            for s in range(cfg.n_sub):
                for c0 in range(0, L, MLP_CHUNK):
                    lc = min(MLP_CHUNK, L - c0)
                    r0 = s * L + c0
                    mixed = _dot(ws_ref[h, :lc, :lc], vnb[r0:r0 + lc])
                    mix_ref[r0:r0 + lc, sl] = (mixed + bs_ref[:lc, sl]).astype(jnp.bfloat16)

    for n in range(N_COL_PIECES):
        @piece(f"a_out{n}")
        def _(n=n):
            cols = col_piece(n)
            k, off = divmod(n * COL_PIECE, ga_width)
            ga = val[f"ga{k}"][:, off:off + COL_PIECE]
            a_out = val["u"][:, cols] * mix_ref[:, cols].astype(jnp.float32) * jax.nn.silu(ga)
            cat_ref[:, cols] = a_out.astype(jnp.bfloat16)

    for g, w in enumerate(POOL_WINDOWS):
        @piece(f"pooling{g}")
        def _(g=g, w=w):
            cols, ccols = col_piece(g), col_piece(g, D_A)
            for s in range(cfg.n_sub):
                rows = slice(s * L, (s + 1) * L)
                acc = z_ref[s, :, cols]
                span = 1
                while span < w:
                    acc = acc + pltpu.roll(acc, span, axis=0)
                    span *= 2
                win = acc[HIST_ROWS:]
                hp_g = val["hp"][rows, cols]
                mix_ref[rows, ccols] = (win * (1.0 / w) - hp_g).astype(jnp.bfloat16)
                if cfg.sequential and s == 0:
                    cnt = jnp.where(j == 0, cnt_ref[:, cols], float(w))
                    head = win[:HIST_ROWS] / cnt - hp_g[:HIST_ROWS]
                    mix_ref[:HIST_ROWS, ccols] = head.astype(jnp.bfloat16)

        @piece(f"b_out{g}")
        def _(g=g):
            cols, ccols = col_piece(g), col_piece(g, D_A)
            q = _dot(mix_ref[:, ccols], pool_w_ref[g])
            b_outp = q * pool_scale_ref[:, cols] * jax.nn.silu(val["gb"][:, cols])
            cat_ref[:, ccols] = b_outp.astype(jnp.bfloat16)

    @piece("w_out")
    def _():
        y = _dot(cat_ref[...], w_out_ref[...]) + b_out_ref[...]
        val["pre"] = ALPHA * xres_ref[...] + y

    @piece("post_norm")
    def _():
        xn = val["xn"] = _layernorm(val["pre"], ln_g_ref[...], ln_b_ref[...])
        val["xnb"] = xn.astype(jnp.bfloat16)
        val["pb"] = p_ref[0].astype(jnp.bfloat16)

    gate_width = D_MODEL // GATE_DOTS
    for n in range(GATE_DOTS):
        @piece(f"gate{n}")
        def _(n=n):
            cols = slice(n * gate_width, (n + 1) * gate_width)
            gate = jax.nn.sigmoid(_dot(val["xnb"], w_pg_ref[:, cols]) + b_pg_ref[:, cols])
            ple = _dot(val["pb"], w_ple_ref[:, cols])
            y_ref[0, :, cols] = val["xn"][:, cols] + gate * ple

    assert set(pieces) == set(STAGE1_PIECES + STAGE2_PIECES)
    for name in (PIPELINE_ORDER if cfg.pipelined else STAGE1_PIECES + STAGE2_PIECES):
        pieces[name]()


def _resident(shape):
    nd = len(shape)
    return pl.BlockSpec(shape, lambda i: (0,) * nd, pipeline_mode=pl.Buffered(1))


def _run_layer(cfg, x, p, hist, cnt, weights, big):
    T = cfg.tile_rows
    n_seq = cfg.n_tiles // cfg.n_inner
    assert x.shape == (n_seq, cfg.n_inner * T, D_MODEL) and p.shape == (n_seq, cfg.n_inner * T, D_PLE)
    assert cfg.sequential or cfg.n_inner == 1
    if cfg.sequential:
        assert hist.shape == (n_seq, HIST_ROWS, D_B)
        hist_spec = pl.BlockSpec((1, HIST_ROWS, D_B), lambda i: (cur(i)[0], 0, 0))
    else:
        assert hist.shape == (POOL_HIST, cfg.n_tiles * cfg.n_sub, D_B) and cfg.n_sub % 8 == 0
        hist_spec = pl.BlockSpec((POOL_HIST, cfg.n_sub, D_B), lambda i: (0, cur(i)[0], 0))

    def cur(i):
        t = jnp.minimum(i, cfg.n_tiles - 1)
        return t // cfg.n_inner, t % cfg.n_inner

    def res(i):
        t = jnp.maximum(i - 1, 0) if cfg.pipelined else i
        return t // cfg.n_inner, t % cfg.n_inner

    in_specs = [
        pl.BlockSpec((1, T, D_MODEL), lambda i: (*cur(i), 0)),
        pl.BlockSpec((1, T, D_PLE), lambda i: (*res(i), 0)),
        hist_spec,
        _resident(cnt.shape),
    ] + [_resident(w.shape) for w in weights] + [pl.BlockSpec(memory_space=pl.ANY) for _ in big]
    out_shape = [jax.ShapeDtypeStruct(x.shape, jnp.float32),
                 jax.ShapeDtypeStruct(hist.shape, jnp.float32)]
    out_specs = [pl.BlockSpec((1, T, D_MODEL), lambda i: (*res(i), 0)), hist_spec]
    if cfg.emit_v:
        out_shape.append(jax.ShapeDtypeStruct(x.shape[:2] + (D_A,), jnp.float32))
        out_specs.append(pl.BlockSpec((1, T, D_A), lambda i: (*cur(i), 0)))
    return pl.pallas_call(
        functools.partial(_layer_kernel, cfg),
        grid=(cfg.n_tiles + (1 if cfg.pipelined else 0),),
        in_specs=in_specs,
        out_specs=out_specs,
        out_shape=out_shape,
        scratch_shapes=[pltpu.VMEM((cfg.n_sub, HIST_ROWS + cfg.sub_len, D_B), jnp.float32),
                        pltpu.VMEM((T, D_MIX), jnp.bfloat16),
                        pltpu.VMEM((T, D_MIX), jnp.bfloat16),
                        pltpu.VMEM((T, D_MODEL), jnp.float32)]
                       + [pltpu.VMEM(s, jnp.bfloat16) for s in BIG_WEIGHT_SHAPES]
                       + [pltpu.VMEM((STAGE_SLOTS, STAGE_ROWS, STAGE_COLS), jnp.float32),
                          pltpu.SemaphoreType.DMA((STAGE_SLOTS,))],
        compiler_params=pltpu.CompilerParams(dimension_semantics=("arbitrary",),
                                             vmem_limit_bytes=VMEM_LIMIT_BYTES),
        name="layer_sample" if cfg.emit_v else "layer_prompt",
    )(x, p, hist, cnt, *weights, *big)


def kernel(x_prompt, x_sample, state_pool, p_prompt, p_sample, w_in, ln_v_g, ln_v_b, w_s, b_s, pool_w, pool_scale, w_out, b_out, ln_g, ln_b, w_ple, w_pg, b_pg):
    assert DEPTH == 1 and w_in.shape[0] == DEPTH
    B, S, _ = x_prompt.shape
    Bs, Ss, _ = x_sample.shape
    bf = jnp.bfloat16
    row = lambda a: a.reshape(1, -1).astype(jnp.float32)

    blk = np.arange(MLP_CHUNK) // CHUNK
    mask = jnp.asarray(blk[None, :] <= blk[:, None])
    ws_masked = jnp.where(mask[None], w_s[0], jnp.zeros((), w_s.dtype)).astype(bf)
    bs_tbl = jnp.repeat(jnp.transpose(b_s[0]), HEAD_A, axis=1).astype(jnp.float32)
    win = np.repeat(np.asarray(POOL_WINDOWS, np.float32), POOL_GROUP)[None, :]
    cnt = jnp.asarray(np.minimum(np.arange(1, HIST_ROWS + 1, dtype=np.float32)[:, None], win))
    vec_rows = [row(ln_v_g[0]), row(ln_v_b[0]), row(pool_scale[0]), row(b_out[0]), row(ln_g[0]), row(ln_b[0]),
                row(b_pg[0])]
    assert len(vec_rows) == N_VECS
    vecs = jnp.concatenate(vec_rows + [jnp.zeros((VEC_ROWS - N_VECS, D_MODEL), jnp.float32)], axis=0)
    weights = (vecs, ws_masked, bs_tbl, pool_w[0].astype(bf))
    big = (w_in[0], w_out[0], w_ple[0], w_pg[0])
    assert tuple(w.shape for w in big) == BIG_WEIGHT_SHAPES and all(w.dtype == jnp.float32 for w in big)

    n_inner = S // PROMPT_TILE_ROWS
    cfg_p = TileConfig(n_sub=1, sub_len=PROMPT_TILE_ROWS, n_inner=n_inner, n_tiles=B * n_inner,
                       sequential=True, pipelined=True, emit_v=False)
    hist_p = jnp.zeros((B, HIST_ROWS, D_B), jnp.float32)
    y_p, hist_p_new = _run_layer(cfg_p, x_prompt, p_prompt[0], hist_p, cnt, weights, big)

    assert PAST_LEN >= POOL_HIST
    ns = SAMPLE_STREAMS_PER_TILE
    cfg_s = TileConfig(n_sub=ns, sub_len=Ss, n_inner=1, n_tiles=Bs // ns,
                       sequential=False, pipelined=False, emit_v=True)
    hist_s = jnp.transpose(state_pool[0], (1, 0, 2))
    y_s, hist_s_new, v_s = _run_layer(
        cfg_s, x_sample.reshape(Bs // ns, ns * Ss, D_MODEL),
        p_sample[0].reshape(Bs // ns, ns * Ss, D_PLE), hist_s, cnt, weights, big)

    return (y_p,
            y_s.reshape(Bs, Ss, D_MODEL),
            hist_p_new[None, :, 1:, :],
            jnp.transpose(hist_s_new, (1, 0, 2))[None],
            v_s.reshape(1, Bs, Ss, N_HEADS_A, HEAD_A))
```

```python
import functools
from typing import NamedTuple

import numpy as np
import jax
import jax.numpy as jnp
from jax.experimental import pallas as pl
from jax.experimental.pallas import tpu as pltpu

D_MODEL = 1024
DEPTH = 1
PAST_LEN = 4096
CHUNK = 64
MLP_CHUNK = 128
D_MIX = 2 * D_MODEL
D_A = D_MIX // 2
HEAD_A = 128
N_HEADS_A = D_A // HEAD_A
D_B = D_MIX - D_A
POOL_WINDOWS = (2, 4, 8, 16)
N_POOL_GROUPS = len(POOL_WINDOWS)
POOL_GROUP = D_B // N_POOL_GROUPS
POOL_HIST = max(POOL_WINDOWS) - 1
HIST_ROWS = POOL_HIST + 1
N_VECS, VEC_ROWS = 7, 8
D_PLE = 256
D_IN = 3 * D_A + 2 * D_B
ALPHA = (2.0 * DEPTH) ** 0.25
LN_EPS = 1e-5
STAGE_ROWS, STAGE_COLS = 256, 1024
STAGE_SLOTS = 8
BIG_WEIGHT_SHAPES = ((D_MODEL, D_IN), (D_MIX, D_MODEL), (D_PLE, D_MODEL), (D_MODEL, D_MODEL))
assert all(k % STAGE_ROWS == 0 and n % STAGE_COLS == 0 for k, n in BIG_WEIGHT_SHAPES)

PROMPT_TILE_ROWS = 512
SAMPLE_STREAMS_PER_TILE = 8
VMEM_LIMIT_BYTES = 56 * 1024 * 1024
COL_PIECE = 256
N_COL_PIECES = D_MODEL // COL_PIECE
assert COL_PIECE == POOL_GROUP and COL_PIECE == 2 * HEAD_A


def _numbered(name, n):
    return tuple(f"{name}{i}" for i in range(n))


GA_DOTS = N_COL_PIECES
GATE_DOTS = N_COL_PIECES
STAGE1_PIECES = (("v_proj", "u_proj") + _numbered("ga_proj", GA_DOTS) + _numbered("ln_v", N_HEADS_A)
                 + _numbered("a_out", N_COL_PIECES) + ("hp_proj", "gb_proj") + _numbered("pooling", N_POOL_GROUPS)
                 + _numbered("b_out", N_POOL_GROUPS))
STAGE2_PIECES = ("w_out", "post_norm") + _numbered("gate", GATE_DOTS)
PIPELINE_ORDER = (
    "v_proj", "w_out", "u_proj",
    "ln_v0", "ln_v1", "ga_proj0", "ln_v2", "ln_v3", "ga_proj1",
    "ln_v4", "ln_v5", "ga_proj2", "ln_v6", "ln_v7", "ga_proj3",
    "post_norm", "a_out0", "a_out1", "a_out2", "a_out3", "hp_proj",
    "pooling0", "pooling1", "pooling2", "pooling3",
    "gate0", "gate1", "gate2", "gate3", "gb_proj",
    "b_out0", "b_out1", "b_out2", "b_out3")
assert sorted(PIPELINE_ORDER) == sorted(STAGE1_PIECES + STAGE2_PIECES)
assert PIPELINE_ORDER.index("w_out") < PIPELINE_ORDER.index("u_proj")


class TileConfig(NamedTuple):
    n_sub: int
    sub_len: int
    n_inner: int
    n_tiles: int
    sequential: bool
    pipelined: bool
    emit_v: bool

    @property
    def tile_rows(self):
        return self.n_sub * self.sub_len


def _layernorm(x, g, b):
    mu = jnp.mean(x, axis=-1, keepdims=True)
    xc = x - mu
    var = jnp.mean(xc * xc, axis=-1, keepdims=True)
    return xc * jax.lax.rsqrt(var + LN_EPS) * g + b


def _dot(a, b):
    return jnp.dot(a, b, preferred_element_type=jnp.float32)


def _history_to_rows(z_ref, s, row0, hist_ref, col):
    for r in range(POOL_HIST):
        z_ref[s, row0 + r:row0 + r + 1, :] = hist_ref[r, col:col + 1, :]


def _rows_to_history(hist_out_ref, col, z_ref, s, row0):
    for r in range(POOL_HIST):
        hist_out_ref[r, col:col + 1, :] = z_ref[s, row0 + r:row0 + r + 1, :]


def _stage_weights(hbm_refs, vmem_refs, stage_ref, sem):
    n_slots = stage_ref.shape[0]
    chunks = [(src, dst, r, c)
              for src, dst in zip(hbm_refs, vmem_refs)
              for r in range(0, src.shape[0], STAGE_ROWS)
              for c in range(0, src.shape[1], STAGE_COLS)]

    def copy(k):
        src, _, r, c = chunks[k]
        return pltpu.make_async_copy(src.at[pl.ds(r, STAGE_ROWS), pl.ds(c, STAGE_COLS)],
                                     stage_ref.at[k % n_slots], sem.at[k % n_slots])

    for k in range(min(n_slots - 1, len(chunks))):
        copy(k).start()
    for k, (_, dst, r, c) in enumerate(chunks):
        if k + n_slots - 1 < len(chunks):
            copy(k + n_slots - 1).start()
        copy(k).wait()
        dst[r:r + STAGE_ROWS, c:c + STAGE_COLS] = stage_ref[k % n_slots].astype(jnp.bfloat16)


def _layer_kernel(cfg, x_ref, p_ref, hist_ref, cnt_ref, vecs_ref, ws_f32_ref, bs_ref, pool_w_f32_ref,
                  w_in_hbm, w_out_hbm, w_ple_hbm, w_pg_hbm, *out_and_scratch):
    (*out_and_scratch, xres_ref, ws_ref, pool_w_ref, w_in_ref, w_out_ref, w_ple_ref, w_pg_ref,
     stage_ref, stage_sem) = out_and_scratch
    lnv_g_ref, lnv_b_ref, pool_scale_ref, b_out_ref, ln_g_ref, ln_b_ref, b_pg_ref = (
        vecs_ref.at[k:k + 1] for k in range(N_VECS))
    if cfg.emit_v:
        y_ref, hist_out_ref, v_out_ref, z_ref, mix_ref, cat_ref = out_and_scratch
    else:
        y_ref, hist_out_ref, z_ref, mix_ref, cat_ref = out_and_scratch
        v_out_ref = None
    L = cfg.sub_len
    last_sub = cfg.n_sub - 1
    step = pl.program_id(0)
    j = jax.lax.rem(jnp.minimum(step, cfg.n_tiles - 1), cfg.n_inner)

    @pl.when(step == 0)
    def _():
        _stage_weights((w_in_hbm, w_out_hbm, w_ple_hbm, w_pg_hbm), (w_in_ref, w_out_ref, w_ple_ref, w_pg_ref),
                       stage_ref, stage_sem)
        q_blk = jax.lax.broadcasted_iota(jnp.int32, ws_f32_ref.shape, 1) // CHUNK
        k_blk = jax.lax.broadcasted_iota(jnp.int32, ws_f32_ref.shape, 2) // CHUNK
        ws_ref[...] = jnp.where(k_blk <= q_blk, ws_f32_ref[...], 0.0).astype(jnp.bfloat16)
        pool_w_ref[...] = pool_w_f32_ref[...].astype(jnp.bfloat16)
        if cfg.pipelined:
            cat_ref[...] = jnp.zeros_like(cat_ref)
            xres_ref[...] = jnp.zeros_like(xres_ref)

    if cfg.sequential:
        @pl.when(j == 0)
        def _():
            z_ref[last_sub, L:, :] = hist_ref[0]

    val = {}
    pieces = {}

    def piece(name):
        def register(fn):
            pieces[name] = fn
        return register

    def col_piece(n, base=0):
        return slice(base + n * COL_PIECE, base + (n + 1) * COL_PIECE)

    def proj(lo, hi):
        return _dot(val["xb"], w_in_ref[:, lo:hi])

    @piece("v_proj")
    def _():
        val["xb"] = x_ref[0].astype(jnp.bfloat16)
        val["v"] = proj(D_A, 2 * D_A)

    @piece("u_proj")
    def _():
        val["u"] = proj(0, D_A)
        xres_ref[...] = x_ref[0]

    ga_width = D_A // GA_DOTS
    for n in range(GA_DOTS):
        @piece(f"ga_proj{n}")
        def _(n=n):
            val[f"ga{n}"] = proj(2 * D_A + n * ga_width, 2 * D_A + (n + 1) * ga_width)

    @piece("hp_proj")
    def _():
        hp = val["hp"] = proj(3 * D_A, 3 * D_A + D_B)
        if cfg.sequential:
            z_ref[0, :HIST_ROWS, :] = z_ref[last_sub, L:, :]
        else:
            z_ref[:, :1, :] = jnp.zeros((cfg.n_sub, 1, D_B), jnp.float32)
            for s in range(cfg.n_sub):
                _history_to_rows(z_ref, s, 1, hist_ref, s)
        for s in range(cfg.n_sub):
            if cfg.sequential and s > 0:
                z_ref[s, :HIST_ROWS, :] = z_ref[s - 1, L:, :]
            z_ref[s, HIST_ROWS:, :] = hp[s * L:(s + 1) * L]
        if cfg.sequential:
            hist_out_ref[0] = z_ref[last_sub, L:, :]
        else:
            for s in range(cfg.n_sub):
                _rows_to_history(hist_out_ref, s, z_ref, s, L + 1)

    @piece("gb_proj")
    def _():
        val["gb"] = proj(3 * D_A + D_B, D_IN)

    for h in range(N_HEADS_A):
        @piece(f"ln_v{h}")
        def _(h=h):
            sl = slice(h * HEAD_A, (h + 1) * HEAD_A)
            vn = _layernorm(val["v"][:, sl], lnv_g_ref[:, sl], lnv_b_ref[:, sl])
            if v_out_ref is not None:
                v_out_ref[0, :, sl] = vn
            vnb = vn.astype(jnp.bfloat16)
            for s in range(cfg.n_sub):
                for c0 in range(0, L, MLP_CHUNK):
                    lc = min(MLP_CHUNK, L - c0)
                    r0 = s * L + c0
                    mixed = _dot(ws_ref[h, :lc, :lc], vnb[r0:r0 + lc])
                    mix_ref[r0:r0 + lc, sl] = (mixed + bs_ref[:lc, sl]).astype(jnp.bfloat16)

    for n in range(N_COL_PIECES):
        @piece(f"a_out{n}")
        def _(n=n):
            cols = col_piece(n)
            k, off = divmod(n * COL_PIECE, ga_width)
            ga = val[f"ga{k}"][:, off:off + COL_PIECE]
            a_out = val["u"][:, cols] * mix_ref[:, cols].astype(jnp.float32) * jax.nn.silu(ga)
            cat_ref[:, cols] = a_out.astype(jnp.bfloat16)

    for g, w in enumerate(POOL_WINDOWS):
        @piece(f"pooling{g}")
        def _(g=g, w=w):
            cols, ccols = col_piece(g), col_piece(g, D_A)
            for s in range(cfg.n_sub):
                rows = slice(s * L, (s + 1) * L)
                acc = z_ref[s, :, cols]
                span = 1
                while span < w:
                    acc = acc + pltpu.roll(acc, span, axis=0)
                    span *= 2
                win = acc[HIST_ROWS:]
                hp_g = val["hp"][rows, cols]
                mix_ref[rows, ccols] = (win * (1.0 / w) - hp_g).astype(jnp.bfloat16)
                if cfg.sequential and s == 0:
                    cnt = jnp.where(j == 0, cnt_ref[:, cols], float(w))
                    head = win[:HIST_ROWS] / cnt - hp_g[:HIST_ROWS]
                    mix_ref[:HIST_ROWS, ccols] = head.astype(jnp.bfloat16)

        @piece(f"b_out{g}")
        def _(g=g):
            cols, ccols = col_piece(g), col_piece(g, D_A)
            q = _dot(mix_ref[:, ccols], pool_w_ref[g])
            b_outp = q * pool_scale_ref[:, cols] * jax.nn.silu(val["gb"][:, cols])
            cat_ref[:, ccols] = b_outp.astype(jnp.bfloat16)

    @piece("w_out")
    def _():
        y = _dot(cat_ref[...], w_out_ref[...]) + b_out_ref[...]
        val["pre"] = ALPHA * xres_ref[...] + y

    @piece("post_norm")
    def _():
        xn = val["xn"] = _layernorm(val["pre"], ln_g_ref[...], ln_b_ref[...])
        val["xnb"] = xn.astype(jnp.bfloat16)
        val["pb"] = p_ref[0].astype(jnp.bfloat16)

    gate_width = D_MODEL // GATE_DOTS
    for n in range(GATE_DOTS):
        @piece(f"gate{n}")
        def _(n=n):
            cols = slice(n * gate_width, (n + 1) * gate_width)
            gate = jax.nn.sigmoid(_dot(val["xnb"], w_pg_ref[:, cols]) + b_pg_ref[:, cols])
            ple = _dot(val["pb"], w_ple_ref[:, cols])
            y_ref[0, :, cols] = val["xn"][:, cols] + gate * ple

    assert set(pieces) == set(STAGE1_PIECES + STAGE2_PIECES)
    for name in (PIPELINE_ORDER if cfg.pipelined else STAGE1_PIECES + STAGE2_PIECES):
        pieces[name]()


def _resident(shape):
    nd = len(shape)
    return pl.BlockSpec(shape, lambda i: (0,) * nd, pipeline_mode=pl.Buffered(1))


def _run_layer(cfg, x, p, hist, cnt, weights, big):
    T = cfg.tile_rows
    n_seq = cfg.n_tiles // cfg.n_inner
    assert x.shape == (n_seq, cfg.n_inner * T, D_MODEL) and p.shape == (n_seq, cfg.n_inner * T, D_PLE)
    assert cfg.sequential or cfg.n_inner == 1
    if cfg.sequential:
        assert hist.shape == (1, HIST_ROWS, D_B)
        hist_in_spec = pl.BlockSpec((1, HIST_ROWS, D_B), lambda i: (0, 0, 0))
        hist_out_shape = (n_seq, HIST_ROWS, D_B)
        hist_spec = pl.BlockSpec((1, HIST_ROWS, D_B), lambda i: (cur(i)[0], 0, 0))
    else:
        assert hist.shape == (POOL_HIST, cfg.n_tiles * cfg.n_sub, D_B) and cfg.n_sub % 8 == 0
        hist_out_shape = hist.shape
        hist_in_spec = hist_spec = pl.BlockSpec((POOL_HIST, cfg.n_sub, D_B), lambda i: (0, cur(i)[0], 0))

    def cur(i):
        t = jnp.minimum(i, cfg.n_tiles - 1)
        return t // cfg.n_inner, t % cfg.n_inner

    def res(i):
        t = jnp.maximum(i - 1, 0) if cfg.pipelined else i
        return t // cfg.n_inner, t % cfg.n_inner

    in_specs = [
        pl.BlockSpec((1, T, D_MODEL), lambda i: (*cur(i), 0)),
        pl.BlockSpec((1, T, D_PLE), lambda i: (*res(i), 0)),
        hist_in_spec,
        _resident(cnt.shape),
    ] + [_resident(w.shape) for w in weights] + [pl.BlockSpec(memory_space=pl.ANY) for _ in big]
    out_shape = [jax.ShapeDtypeStruct(x.shape, jnp.float32),
                 jax.ShapeDtypeStruct(hist_out_shape, jnp.float32)]
    out_specs = [pl.BlockSpec((1, T, D_MODEL), lambda i: (*res(i), 0)), hist_spec]
    if cfg.emit_v:
        out_shape.append(jax.ShapeDtypeStruct(x.shape[:2] + (D_A,), jnp.float32))
        out_specs.append(pl.BlockSpec((1, T, D_A), lambda i: (*cur(i), 0)))
    return pl.pallas_call(
        functools.partial(_layer_kernel, cfg),
        grid=(cfg.n_tiles + (1 if cfg.pipelined else 0),),
        in_specs=in_specs,
        out_specs=out_specs,
        out_shape=out_shape,
        scratch_shapes=[pltpu.VMEM((cfg.n_sub, HIST_ROWS + cfg.sub_len, D_B), jnp.float32),
                        pltpu.VMEM((T, D_MIX), jnp.bfloat16),
                        pltpu.VMEM((T, D_MIX), jnp.bfloat16),
                        pltpu.VMEM((T, D_MODEL), jnp.float32),
                        pltpu.VMEM((N_HEADS_A, MLP_CHUNK, MLP_CHUNK), jnp.bfloat16),
                        pltpu.VMEM((N_POOL_GROUPS, POOL_GROUP, POOL_GROUP), jnp.bfloat16)]
                       + [pltpu.VMEM(s, jnp.bfloat16) for s in BIG_WEIGHT_SHAPES]
                       + [pltpu.VMEM((STAGE_SLOTS, STAGE_ROWS, STAGE_COLS), jnp.float32),
                          pltpu.SemaphoreType.DMA((STAGE_SLOTS,))],
        compiler_params=pltpu.CompilerParams(dimension_semantics=("arbitrary",),
                                             vmem_limit_bytes=VMEM_LIMIT_BYTES),
        name="layer_sample" if cfg.emit_v else "layer_prompt",
    )(x, p, hist, cnt, *weights, *big)


def kernel(x_prompt, x_sample, state_pool, p_prompt, p_sample, w_in, ln_v_g, ln_v_b, w_s, b_s, pool_w, pool_scale, w_out, b_out, ln_g, ln_b, w_ple, w_pg, b_pg):
    assert DEPTH == 1 and w_in.shape[0] == DEPTH
    B, S, _ = x_prompt.shape
    Bs, Ss, _ = x_sample.shape
    row = lambda a: a.reshape(1, -1).astype(jnp.float32)

    bs_tbl = jnp.repeat(jnp.transpose(b_s[0]), HEAD_A, axis=1).astype(jnp.float32)
    win = np.repeat(np.asarray(POOL_WINDOWS, np.float32), POOL_GROUP)[None, :]
    cnt = jnp.asarray(np.minimum(np.arange(1, HIST_ROWS + 1, dtype=np.float32)[:, None], win))
    vec_rows = [row(ln_v_g[0]), row(ln_v_b[0]), row(pool_scale[0]), row(b_out[0]), row(ln_g[0]), row(ln_b[0]),
                row(b_pg[0])]
    assert len(vec_rows) == N_VECS
    vecs = jnp.concatenate(vec_rows + [jnp.zeros((VEC_ROWS - N_VECS, D_MODEL), jnp.float32)], axis=0)
    weights = (vecs, w_s[0], bs_tbl, pool_w[0])
    big = (w_in[0], w_out[0], w_ple[0], w_pg[0])
    assert tuple(w.shape for w in big) == BIG_WEIGHT_SHAPES and all(w.dtype == jnp.float32 for w in big)

    n_inner = S // PROMPT_TILE_ROWS
    cfg_p = TileConfig(n_sub=1, sub_len=PROMPT_TILE_ROWS, n_inner=n_inner, n_tiles=B * n_inner,
                       sequential=True, pipelined=True, emit_v=False)
    hist_p = jnp.zeros((1, HIST_ROWS, D_B), jnp.float32)
    y_p, hist_p_new = _run_layer(cfg_p, x_prompt, p_prompt[0], hist_p, cnt, weights, big)

    assert PAST_LEN >= POOL_HIST
    ns = SAMPLE_STREAMS_PER_TILE
    cfg_s = TileConfig(n_sub=ns, sub_len=Ss, n_inner=1, n_tiles=Bs // ns,
                       sequential=False, pipelined=False, emit_v=True)
    hist_s = jnp.transpose(state_pool[0], (1, 0, 2))
    y_s, hist_s_new, v_s = _run_layer(
        cfg_s, x_sample.reshape(Bs // ns, ns * Ss, D_MODEL),
        p_sample[0].reshape(Bs // ns, ns * Ss, D_PLE), hist_s, cnt, weights, big)

    return (y_p,
            y_s.reshape(Bs, Ss, D_MODEL),
            hist_p_new[None, :, 1:, :],
            jnp.transpose(hist_s_new, (1, 0, 2))[None],
            v_s.reshape(1, Bs, Ss, N_HEADS_A, HEAD_A))
```

```python
import functools
from typing import NamedTuple

import numpy as np
import jax
import jax.numpy as jnp
from jax.experimental import pallas as pl
from jax.experimental.pallas import tpu as pltpu

D_MODEL = 1024
DEPTH = 1
PAST_LEN = 4096
CHUNK = 64
MLP_CHUNK = 128
D_MIX = 2 * D_MODEL
D_A = D_MIX // 2
HEAD_A = 128
N_HEADS_A = D_A // HEAD_A
D_B = D_MIX - D_A
POOL_WINDOWS = (2, 4, 8, 16)
N_POOL_GROUPS = len(POOL_WINDOWS)
POOL_GROUP = D_B // N_POOL_GROUPS
POOL_HIST = max(POOL_WINDOWS) - 1
HIST_ROWS = POOL_HIST + 1
N_VECS, VEC_ROWS = 7, 8
D_PLE = 256
D_IN = 3 * D_A + 2 * D_B
ALPHA = (2.0 * DEPTH) ** 0.25
LN_EPS = 1e-5
STAGE_ROWS, STAGE_COLS = 256, 1024
STAGE_SLOTS = 8
BIG_WEIGHT_SHAPES = ((D_MODEL, D_IN), (D_MIX, D_MODEL), (D_PLE, D_MODEL), (D_MODEL, D_MODEL))
assert all(k % STAGE_ROWS == 0 and n % STAGE_COLS == 0 for k, n in BIG_WEIGHT_SHAPES)

PROMPT_TILE_ROWS = 512
SAMPLE_STREAMS_PER_TILE = 8
VMEM_LIMIT_BYTES = 56 * 1024 * 1024
COL_PIECE = 256
N_COL_PIECES = D_MODEL // COL_PIECE
assert COL_PIECE == POOL_GROUP and COL_PIECE == 2 * HEAD_A


def _numbered(name, n):
    return tuple(f"{name}{i}" for i in range(n))


GA_DOTS = N_COL_PIECES
GATE_DOTS = N_COL_PIECES
STAGE1_PIECES = (("v_proj", "u_proj") + _numbered("ga_proj", GA_DOTS) + _numbered("ln_v", N_HEADS_A)
                 + _numbered("a_out", N_COL_PIECES) + ("hp_proj", "gb_proj") + _numbered("pooling", N_POOL_GROUPS)
                 + _numbered("b_out", N_POOL_GROUPS))
STAGE2_PIECES = ("w_out", "post_norm") + _numbered("gate", GATE_DOTS)
PIPELINE_ORDER = (
    "v_proj", "w_out", "u_proj",
    "ln_v0", "ln_v1", "ga_proj0", "ln_v2", "ln_v3", "ga_proj1",
    "ln_v4", "ln_v5", "ga_proj2", "ln_v6", "ln_v7", "ga_proj3",
    "post_norm", "a_out0", "a_out1", "a_out2", "a_out3", "hp_proj",
    "pooling0", "pooling1", "pooling2", "pooling3",
    "gate0", "gate1", "gate2", "gate3", "gb_proj",
    "b_out0", "b_out1", "b_out2", "b_out3")
assert sorted(PIPELINE_ORDER) == sorted(STAGE1_PIECES + STAGE2_PIECES)
assert PIPELINE_ORDER.index("w_out") < PIPELINE_ORDER.index("u_proj")


class TileConfig(NamedTuple):
    n_sub: int
    sub_len: int
    n_inner: int
    n_tiles: int
    sequential: bool
    pipelined: bool
    emit_v: bool

    @property
    def tile_rows(self):
        return self.n_sub * self.sub_len


def _layernorm(x, g, b):
    mu = jnp.mean(x, axis=-1, keepdims=True)
    xc = x - mu
    var = jnp.mean(xc * xc, axis=-1, keepdims=True)
    return xc * jax.lax.rsqrt(var + LN_EPS) * g + b


def _dot(a, b):
    return jnp.dot(a, b, preferred_element_type=jnp.float32)


def _history_to_rows(z_ref, s, row0, hist_ref, col):
    for r in range(POOL_HIST):
        z_ref[s, row0 + r:row0 + r + 1, :] = hist_ref[r, col:col + 1, :]


def _rows_to_history(hist_out_ref, col, z_ref, s, row0):
    for r in range(POOL_HIST):
        hist_out_ref[r, col:col + 1, :] = z_ref[s, row0 + r:row0 + r + 1, :]


def _stage_weights(hbm_refs, vmem_refs, stage_ref, sem):
    n_slots = stage_ref.shape[0]
    chunks = [(src, dst, r, c)
              for src, dst in zip(hbm_refs, vmem_refs)
              for r in range(0, src.shape[0], STAGE_ROWS)
              for c in range(0, src.shape[1], STAGE_COLS)]

    def copy(k):
        src, _, r, c = chunks[k]
        return pltpu.make_async_copy(src.at[pl.ds(r, STAGE_ROWS), pl.ds(c, STAGE_COLS)],
                                     stage_ref.at[k % n_slots], sem.at[k % n_slots])

    for k in range(min(n_slots - 1, len(chunks))):
        copy(k).start()
    for k, (_, dst, r, c) in enumerate(chunks):
        if k + n_slots - 1 < len(chunks):
            copy(k + n_slots - 1).start()
        copy(k).wait()
        dst[r:r + STAGE_ROWS, c:c + STAGE_COLS] = stage_ref[k % n_slots].astype(jnp.bfloat16)


def _layer_kernel(cfg, x_ref, p_ref, hist_ref, cnt_ref, vecs_ref, ws_f32_ref, pool_w_f32_ref,
                  w_in_hbm, w_out_hbm, w_ple_hbm, w_pg_hbm, *out_and_scratch):
    (*out_and_scratch, xres_ref, ws_ref, pool_w_ref, bs_ref, w_in_ref, w_out_ref, w_ple_ref, w_pg_ref,
     stage_ref, stage_sem) = out_and_scratch
    lnv_g_ref, lnv_b_ref, pool_scale_ref, b_out_ref, ln_g_ref, ln_b_ref, b_pg_ref = (
        vecs_ref.at[k:k + 1] for k in range(N_VECS))
    if cfg.emit_v:
        y_ref, hist_out_ref, v_out_ref, z_ref, mix_ref, cat_ref = out_and_scratch
    else:
        y_ref, hist_out_ref, z_ref, mix_ref, cat_ref = out_and_scratch
        v_out_ref = None
    L = cfg.sub_len
    last_sub = cfg.n_sub - 1
    step = pl.program_id(0)
    j = jax.lax.rem(jnp.minimum(step, cfg.n_tiles - 1), cfg.n_inner)

    @pl.when(step == 0)
    def _():
        _stage_weights((w_in_hbm, w_out_hbm, w_ple_hbm, w_pg_hbm), (w_in_ref, w_out_ref, w_ple_ref, w_pg_ref),
                       stage_ref, stage_sem)
        q_blk = jax.lax.broadcasted_iota(jnp.int32, ws_f32_ref.shape, 1) // CHUNK
        k_blk = jax.lax.broadcasted_iota(jnp.int32, ws_f32_ref.shape, 2) // CHUNK
        ws_ref[...] = jnp.where(k_blk <= q_blk, ws_f32_ref[...], 0.0).astype(jnp.bfloat16)
        pool_w_ref[...] = pool_w_f32_ref[...].astype(jnp.bfloat16)
        for h in range(N_HEADS_A):
            sl = slice(h * HEAD_A, (h + 1) * HEAD_A)
            bs_ref[:, sl] = jnp.transpose(jnp.broadcast_to(vecs_ref[N_VECS:N_VECS + 1, sl], (HEAD_A, MLP_CHUNK)))
        if cfg.pipelined:
            cat_ref[...] = jnp.zeros_like(cat_ref)
            xres_ref[...] = jnp.zeros_like(xres_ref)

    if cfg.sequential:
        @pl.when(j == 0)
        def _():
            z_ref[last_sub, L:, :] = hist_ref[0]

    val = {}
    pieces = {}

    def piece(name):
        def register(fn):
            pieces[name] = fn
        return register

    def col_piece(n, base=0):
        return slice(base + n * COL_PIECE, base + (n + 1) * COL_PIECE)

    def proj(lo, hi):
        return _dot(val["xb"], w_in_ref[:, lo:hi])

    @piece("v_proj")
    def _():
        val["xb"] = x_ref[0].astype(jnp.bfloat16)
        val["v"] = proj(D_A, 2 * D_A)

    @piece("u_proj")
    def _():
        val["u"] = proj(0, D_A)
        xres_ref[...] = x_ref[0]

    ga_width = D_A // GA_DOTS
    for n in range(GA_DOTS):
        @piece(f"ga_proj{n}")
        def _(n=n):
            val[f"ga{n}"] = proj(2 * D_A + n * ga_width, 2 * D_A + (n + 1) * ga_width)

    @piece("hp_proj")
    def _():
        hp = val["hp"] = proj(3 * D_A, 3 * D_A + D_B)
        if cfg.sequential:
            z_ref[0, :HIST_ROWS, :] = z_ref[last_sub, L:, :]
        else:
            z_ref[:, :1, :] = jnp.zeros((cfg.n_sub, 1, D_B), jnp.float32)
            for s in range(cfg.n_sub):
                _history_to_rows(z_ref, s, 1, hist_ref, s)
        for s in range(cfg.n_sub):
            if cfg.sequential and s > 0:
                z_ref[s, :HIST_ROWS, :] = z_ref[s - 1, L:, :]
            z_ref[s, HIST_ROWS:, :] = hp[s * L:(s + 1) * L]
        if cfg.sequential:
            hist_out_ref[0] = z_ref[last_sub, L:, :]
        else:
            for s in range(cfg.n_sub):
                _rows_to_history(hist_out_ref, s, z_ref, s, L + 1)

    @piece("gb_proj")
    def _():
        val["gb"] = proj(3 * D_A + D_B, D_IN)

    for h in range(N_HEADS_A):
        @piece(f"ln_v{h}")
        def _(h=h):
            sl = slice(h * HEAD_A, (h + 1) * HEAD_A)
            vn = _layernorm(val["v"][:, sl], lnv_g_ref[:, sl], lnv_b_ref[:, sl])
            if v_out_ref is not None:
                v_out_ref[0, :, sl] = vn
            vnb = vn.astype(jnp.bfloat16)
            for s in range(cfg.n_sub):
                for c0 in range(0, L, MLP_CHUNK):
                    lc = min(MLP_CHUNK, L - c0)
                    r0 = s * L + c0
                    mixed = _dot(ws_ref[h, :lc, :lc], vnb[r0:r0 + lc])
                    mix_ref[r0:r0 + lc, sl] = (mixed + bs_ref[:lc, sl]).astype(jnp.bfloat16)

    for n in range(N_COL_PIECES):
        @piece(f"a_out{n}")
        def _(n=n):
            cols = col_piece(n)
            k, off = divmod(n * COL_PIECE, ga_width)
            ga = val[f"ga{k}"][:, off:off + COL_PIECE]
            a_out = val["u"][:, cols] * mix_ref[:, cols].astype(jnp.float32) * jax.nn.silu(ga)
            cat_ref[:, cols] = a_out.astype(jnp.bfloat16)

    for g, w in enumerate(POOL_WINDOWS):
        @piece(f"pooling{g}")
        def _(g=g, w=w):
            cols, ccols = col_piece(g), col_piece(g, D_A)
            for s in range(cfg.n_sub):
                rows = slice(s * L, (s + 1) * L)
                acc = z_ref[s, :, cols]
                span = 1
                while span < w:
                    acc = acc + pltpu.roll(acc, span, axis=0)
                    span *= 2
                win = acc[HIST_ROWS:]
                hp_g = val["hp"][rows, cols]
                mix_ref[rows, ccols] = (win * (1.0 / w) - hp_g).astype(jnp.bfloat16)
                if cfg.sequential and s == 0:
                    cnt = jnp.where(j == 0, cnt_ref[:, cols], float(w))
                    head = win[:HIST_ROWS] / cnt - hp_g[:HIST_ROWS]
                    mix_ref[:HIST_ROWS, ccols] = head.astype(jnp.bfloat16)

        @piece(f"b_out{g}")
        def _(g=g):
            cols, ccols = col_piece(g), col_piece(g, D_A)
            q = _dot(mix_ref[:, ccols], pool_w_ref[g])
            b_outp = q * pool_scale_ref[:, cols] * jax.nn.silu(val["gb"][:, cols])
            cat_ref[:, ccols] = b_outp.astype(jnp.bfloat16)

    @piece("w_out")
    def _():
        y = _dot(cat_ref[...], w_out_ref[...]) + b_out_ref[...]
        val["pre"] = ALPHA * xres_ref[...] + y

    @piece("post_norm")
    def _():
        xn = val["xn"] = _layernorm(val["pre"], ln_g_ref[...], ln_b_ref[...])
        val["xnb"] = xn.astype(jnp.bfloat16)
        val["pb"] = p_ref[0].astype(jnp.bfloat16)

    gate_width = D_MODEL // GATE_DOTS
    for n in range(GATE_DOTS):
        @piece(f"gate{n}")
        def _(n=n):
            cols = slice(n * gate_width, (n + 1) * gate_width)
            gate = jax.nn.sigmoid(_dot(val["xnb"], w_pg_ref[:, cols]) + b_pg_ref[:, cols])
            ple = _dot(val["pb"], w_ple_ref[:, cols])
            y_ref[0, :, cols] = val["xn"][:, cols] + gate * ple

    assert set(pieces) == set(STAGE1_PIECES + STAGE2_PIECES)
    for name in (PIPELINE_ORDER if cfg.pipelined else STAGE1_PIECES + STAGE2_PIECES):
        pieces[name]()


def _resident(shape):
    nd = len(shape)
    return pl.BlockSpec(shape, lambda i: (0,) * nd, pipeline_mode=pl.Buffered(1))


def _run_layer(cfg, x, p, hist, cnt, weights, big):
    T = cfg.tile_rows
    n_seq = cfg.n_tiles // cfg.n_inner
    assert x.shape == (n_seq, cfg.n_inner * T, D_MODEL) and p.shape == (n_seq, cfg.n_inner * T, D_PLE)
    assert cfg.sequential or cfg.n_inner == 1
    if cfg.sequential:
        assert hist.shape == (1, HIST_ROWS, D_B)
        hist_in_spec = pl.BlockSpec((1, HIST_ROWS, D_B), lambda i: (0, 0, 0))
        hist_out_shape = (n_seq, HIST_ROWS, D_B)
        hist_spec = pl.BlockSpec((1, HIST_ROWS, D_B), lambda i: (cur(i)[0], 0, 0))
    else:
        assert hist.shape == (POOL_HIST, cfg.n_tiles * cfg.n_sub, D_B) and cfg.n_sub % 8 == 0
        hist_out_shape = hist.shape
        hist_in_spec = hist_spec = pl.BlockSpec((POOL_HIST, cfg.n_sub, D_B), lambda i: (0, cur(i)[0], 0))

    def cur(i):
        t = jnp.minimum(i, cfg.n_tiles - 1)
        return t // cfg.n_inner, t % cfg.n_inner

    def res(i):
        t = jnp.maximum(i - 1, 0) if cfg.pipelined else i
        return t // cfg.n_inner, t % cfg.n_inner

    in_specs = [
        pl.BlockSpec((1, T, D_MODEL), lambda i: (*cur(i), 0)),
        pl.BlockSpec((1, T, D_PLE), lambda i: (*res(i), 0)),
        hist_in_spec,
        _resident(cnt.shape),
    ] + [_resident(w.shape) for w in weights] + [pl.BlockSpec(memory_space=pl.ANY) for _ in big]
    out_shape = [jax.ShapeDtypeStruct(x.shape, jnp.float32),
                 jax.ShapeDtypeStruct(hist_out_shape, jnp.float32)]
    out_specs = [pl.BlockSpec((1, T, D_MODEL), lambda i: (*res(i), 0)), hist_spec]
    if cfg.emit_v:
        out_shape.append(jax.ShapeDtypeStruct(x.shape[:2] + (D_A,), jnp.float32))
        out_specs.append(pl.BlockSpec((1, T, D_A), lambda i: (*cur(i), 0)))
    return pl.pallas_call(
        functools.partial(_layer_kernel, cfg),
        grid=(cfg.n_tiles + (1 if cfg.pipelined else 0),),
        in_specs=in_specs,
        out_specs=out_specs,
        out_shape=out_shape,
        scratch_shapes=[pltpu.VMEM((cfg.n_sub, HIST_ROWS + cfg.sub_len, D_B), jnp.float32),
                        pltpu.VMEM((T, D_MIX), jnp.bfloat16),
                        pltpu.VMEM((T, D_MIX), jnp.bfloat16),
                        pltpu.VMEM((T, D_MODEL), jnp.float32),
                        pltpu.VMEM((N_HEADS_A, MLP_CHUNK, MLP_CHUNK), jnp.bfloat16),
                        pltpu.VMEM((N_POOL_GROUPS, POOL_GROUP, POOL_GROUP), jnp.bfloat16),
                        pltpu.VMEM((MLP_CHUNK, D_A), jnp.float32)]
                       + [pltpu.VMEM(s, jnp.bfloat16) for s in BIG_WEIGHT_SHAPES]
                       + [pltpu.VMEM((STAGE_SLOTS, STAGE_ROWS, STAGE_COLS), jnp.float32),
                          pltpu.SemaphoreType.DMA((STAGE_SLOTS,))],
        compiler_params=pltpu.CompilerParams(dimension_semantics=("arbitrary",),
                                             vmem_limit_bytes=VMEM_LIMIT_BYTES),
        name="layer_sample" if cfg.emit_v else "layer_prompt",
    )(x, p, hist, cnt, *weights, *big)


def kernel(x_prompt, x_sample, state_pool, p_prompt, p_sample, w_in, ln_v_g, ln_v_b, w_s, b_s, pool_w, pool_scale, w_out, b_out, ln_g, ln_b, w_ple, w_pg, b_pg):
    assert DEPTH == 1 and w_in.shape[0] == DEPTH
    B, S, _ = x_prompt.shape
    Bs, Ss, _ = x_sample.shape
    row = lambda a: a.reshape(1, -1).astype(jnp.float32)

    win = np.repeat(np.asarray(POOL_WINDOWS, np.float32), POOL_GROUP)[None, :]
    cnt = jnp.asarray(np.minimum(np.arange(1, HIST_ROWS + 1, dtype=np.float32)[:, None], win))
    vec_rows = [row(ln_v_g[0]), row(ln_v_b[0]), row(pool_scale[0]), row(b_out[0]), row(ln_g[0]), row(ln_b[0]),
                row(b_pg[0])]
    assert len(vec_rows) == N_VECS and VEC_ROWS == N_VECS + 1 and D_A == D_MODEL
    vecs = jnp.concatenate(vec_rows + [row(b_s[0])], axis=0)
    weights = (vecs, w_s[0], pool_w[0])
    big = (w_in[0], w_out[0], w_ple[0], w_pg[0])
    assert tuple(w.shape for w in big) == BIG_WEIGHT_SHAPES and all(w.dtype == jnp.float32 for w in big)

    n_inner = S // PROMPT_TILE_ROWS
    cfg_p = TileConfig(n_sub=1, sub_len=PROMPT_TILE_ROWS, n_inner=n_inner, n_tiles=B * n_inner,
                       sequential=True, pipelined=True, emit_v=False)
    hist_p = jnp.zeros((1, HIST_ROWS, D_B), jnp.float32)
    y_p, hist_p_new = _run_layer(cfg_p, x_prompt, p_prompt[0], hist_p, cnt, weights, big)

    assert PAST_LEN >= POOL_HIST
    ns = SAMPLE_STREAMS_PER_TILE
    cfg_s = TileConfig(n_sub=ns, sub_len=Ss, n_inner=1, n_tiles=Bs // ns,
                       sequential=False, pipelined=False, emit_v=True)
    hist_s = jnp.transpose(state_pool[0], (1, 0, 2))
    y_s, hist_s_new, v_s = _run_layer(
        cfg_s, x_sample.reshape(Bs // ns, ns * Ss, D_MODEL),
        p_sample[0].reshape(Bs // ns, ns * Ss, D_PLE), hist_s, cnt, weights, big)

    return (y_p,
            y_s.reshape(Bs, Ss, D_MODEL),
            hist_p_new[None, :, 1:, :],
            jnp.transpose(hist_s_new, (1, 0, 2))[None],
            v_s.reshape(1, Bs, Ss, N_HEADS_A, HEAD_A))
```
